```python
import math
import jax, jax.numpy as jnp
from jax import lax
import numpy as np

D_MODEL = 1024
BATCH = 4
SEQ = 8192
DEPTH = 2

N_MIXERS = 2
N_SB_LAYERS = (DEPTH + 1) // 2
N_GDN_LAYERS = DEPTH // 2
PLE_DIM = 256
SB_HEADS = 16
SB_HEAD_DIM = D_MODEL // SB_HEADS
Q_BLOCK = 128
GDN_QK_HEADS = 8
GDN_V_HEADS = 16
GDN_HEAD_DIM = 128
GDN_KEY_DIM = GDN_QK_HEADS * GDN_HEAD_DIM
GDN_VAL_DIM = GDN_V_HEADS * GDN_HEAD_DIM
GDN_IN_DIM = 2 * GDN_KEY_DIM + 2 * GDN_VAL_DIM + 2 * GDN_V_HEADS
CONV_K = 4
CHUNK = 64
N_EXPERTS = 16
N_GROUPS = 4
EXPERTS_PER_GROUP = N_EXPERTS // N_GROUPS
TOP_K = 2
D_EXPERT = 512
LN_EPS = 1e-5
RMS_EPS = 1e-6
DEEPNORM_ALPHA = (2 * DEPTH) ** 0.25
DEEPNORM_BETA = (8 * DEPTH) ** -0.25

kernel_name = "hybrid_stickbreak_gdn_grouped_moe"


def layer_norm(x, g, b):
    xf = x.astype(jnp.float32)
    mu = xf.mean(-1, keepdims=True)
    var = jnp.square(xf - mu).mean(-1, keepdims=True)
    return ((xf - mu) * lax.rsqrt(var + LN_EPS) * g + b).astype(x.dtype)


def l2_normalize(x):
    xf = x.astype(jnp.float32)
    return xf * lax.rsqrt(jnp.sum(xf * xf, -1, keepdims=True) + RMS_EPS)


def stick_breaking_attention(x, w_in, w_out):
    bsz, t, _ = x.shape
    q, k, v = jnp.split(x @ w_in, 3, axis=-1)
    q = q.reshape(bsz, t, SB_HEADS, SB_HEAD_DIM)
    k = k.reshape(bsz, t, SB_HEADS, SB_HEAD_DIM)
    v = v.reshape(bsz, t, SB_HEADS, SB_HEAD_DIM)
    scale = SB_HEAD_DIM ** -0.5
    outs = []
    for blk in range(t // Q_BLOCK):
        q0, q1 = blk * Q_BLOCK, (blk + 1) * Q_BLOCK
        z = jnp.einsum('bqhd,bkhd->bhqk', q[:, q0:q1], k[:, :q1]).astype(jnp.float32) * scale
        t_idx = q0 + jnp.arange(Q_BLOCK)[:, None]
        s_idx = jnp.arange(q1)[None, :]
        causal = s_idx < t_idx
        log_1m_beta = jnp.where(causal, jax.nn.log_sigmoid(-z), 0.0)
        log_rest = lax.cumsum(log_1m_beta, axis=3, reverse=True) - log_1m_beta
        attn = jnp.where(causal, jnp.exp(jax.nn.log_sigmoid(z) + log_rest), 0.0)
        outs.append(jnp.einsum('bhqk,bkhd->bqhd', attn.astype(v.dtype), v[:, :q1]))
    o = jnp.concatenate(outs, axis=1).reshape(bsz, t, D_MODEL)
    return o @ w_out


def causal_depthwise_conv(x, w):
    return lax.conv_general_dilated(
        x, w[:, None, :], window_strides=(1,), padding=[(CONV_K - 1, 0)],
        dimension_numbers=('NWC', 'WIO', 'NWC'), feature_group_count=x.shape[-1])


def gated_delta_rule(q, k, v, g, beta):
    bsz, nh, t, dk = q.shape
    dv = v.shape[-1]
    n = t // CHUNK
    q = q * dk ** -0.5
    kb = k * beta[..., None]
    vb = v * beta[..., None]
    rs = lambda a: a.reshape(bsz, nh, n, CHUNK, *a.shape[3:])
    q, k, kb, vb, g = rs(q), rs(k), rs(kb), rs(vb), rs(g)
    g = jnp.cumsum(g, axis=-1)
    idx = jnp.arange(CHUNK)
    lower_incl = idx[:, None] >= idx[None, :]
    strict = idx[:, None] > idx[None, :]
    decay = jnp.exp(jnp.where(lower_incl, g[..., :, None] - g[..., None, :], -jnp.inf))
    l_mat = jnp.where(strict, jnp.einsum('bhncd,bhnsd->bhncs', kb, k) * decay, 0.0)
    eye = jnp.eye(CHUNK, dtype=jnp.float32)
    t_inv = lax.linalg.triangular_solve(eye + l_mat, jnp.broadcast_to(eye, l_mat.shape),
                                        left_side=True, lower=True)
    u = t_inv @ vb
    w = t_inv @ (kb * jnp.exp(g)[..., None])
    attn_intra = jnp.where(lower_incl, jnp.einsum('bhncd,bhnsd->bhncs', q, k) * decay, 0.0)

    def step(state, inp):
        q_c, k_c, u_c, w_c, g_c, a_c = inp
        v_new = u_c - w_c @ state
        o = (q_c * jnp.exp(g_c)[..., None]) @ state + a_c @ v_new
        g_last = g_c[..., -1]
        state = state * jnp.exp(g_last)[..., None, None] + jnp.einsum(
            'bhcd,bhce->bhde', k_c * jnp.exp(g_last[..., None] - g_c)[..., None], v_new)
        return state, o

    xs = tuple(jnp.moveaxis(a, 2, 0) for a in (q, k, u, w, g, attn_intra))
    state0 = jnp.zeros((bsz, nh, dk, dv), jnp.float32)
    _, o = lax.scan(step, state0, xs)
    return jnp.moveaxis(o, 0, 2).reshape(bsz, nh, t, dv)


def gated_deltanet(x, w_in, conv_w, a_log, dt_bias, norm_w, w_out):
    bsz, t, _ = x.shape
    proj = x @ w_in
    qkv, z, b_logit, a_logit = jnp.split(
        proj, [2 * GDN_KEY_DIM + GDN_VAL_DIM, 2 * GDN_KEY_DIM + 2 * GDN_VAL_DIM,
               2 * GDN_KEY_DIM + 2 * GDN_VAL_DIM + GDN_V_HEADS], axis=-1)
    qkv = jax.nn.silu(causal_depthwise_conv(qkv, conv_w))
    q, k, v = jnp.split(qkv, [GDN_KEY_DIM, 2 * GDN_KEY_DIM], axis=-1)
    rep = GDN_V_HEADS // GDN_QK_HEADS
    q = jnp.repeat(l2_normalize(q.reshape(bsz, t, GDN_QK_HEADS, GDN_HEAD_DIM)), rep, axis=2)
    k = jnp.repeat(l2_normalize(k.reshape(bsz, t, GDN_QK_HEADS, GDN_HEAD_DIM)), rep, axis=2)
    v = v.reshape(bsz, t, GDN_V_HEADS, GDN_HEAD_DIM).astype(jnp.float32)
    beta = jax.nn.sigmoid(b_logit.astype(jnp.float32))
    g = -jnp.exp(a_log.astype(jnp.float32)) * jax.nn.softplus(
        a_logit.astype(jnp.float32) + dt_bias.astype(jnp.float32))
    tr = lambda a: jnp.swapaxes(a, 1, 2)
    o = gated_delta_rule(tr(q), tr(k), tr(v), tr(g), tr(beta))
    o = tr(o)
    o = o * lax.rsqrt(jnp.mean(o * o, -1, keepdims=True) + RMS_EPS) * norm_w
    o = o * jax.nn.silu(z.reshape(bsz, t, GDN_V_HEADS, GDN_HEAD_DIM).astype(jnp.float32))
    return o.reshape(bsz, t, GDN_VAL_DIM).astype(x.dtype) @ w_out


def grouped_moe(h, router_w, router_b, w_gate_up, w_down):
    bsz, t, d = h.shape
    ht = h.reshape(-1, d)
    scores = jax.nn.sigmoid((ht @ router_w).astype(jnp.float32))
    sel = scores + router_b.astype(jnp.float32)
    grp_score = lax.top_k(sel.reshape(-1, N_GROUPS, EXPERTS_PER_GROUP), TOP_K)[0].sum(-1)
    grp_mask = jax.nn.one_hot(jnp.argmax(grp_score, -1), N_GROUPS, dtype=jnp.bool_)
    masked = jnp.where(jnp.repeat(grp_mask, EXPERTS_PER_GROUP, axis=-1), sel, -jnp.inf)
    _, top_idx = lax.top_k(masked, TOP_K)
    top_scores = jnp.take_along_axis(scores, top_idx, axis=-1)
    weights = top_scores / jnp.sum(top_scores, -1, keepdims=True)
    gate = jnp.sum(jax.nn.one_hot(top_idx, N_EXPERTS, dtype=jnp.float32) * weights[..., None], axis=1)
    out = jnp.zeros_like(ht)
    for e in range(N_EXPERTS):
        g_in, u_in = jnp.split(ht @ w_gate_up[e], 2, axis=-1)
        out = out + gate[:, e:e + 1].astype(ht.dtype) * ((jax.nn.silu(g_in) * u_in) @ w_down[e])
    return out.reshape(bsz, t, d)


def setup_inputs(seed: int = 0) -> dict:
    key = jax.random.key(seed)
    ks = jax.random.split(key, 24)
    nrm = lambda k, shape, s: jax.random.normal(k, shape, jnp.float32) * s
    dt = jax.random.uniform(ks[7], (N_GDN_LAYERS, GDN_V_HEADS), jnp.float32, 0.001, 0.1)
    return {
        "x": nrm(ks[0], (BATCH, SEQ, D_MODEL), 1.0),
        "p": nrm(ks[1], (DEPTH, BATCH, SEQ, PLE_DIM), 1.0),
        "sb_w_in": nrm(ks[2], (N_SB_LAYERS, D_MODEL, 3 * D_MODEL), D_MODEL ** -0.5),
        "sb_w_out": nrm(ks[3], (N_SB_LAYERS, D_MODEL, D_MODEL), D_MODEL ** -0.5 * DEEPNORM_BETA),
        "gdn_w_in": nrm(ks[4], (N_GDN_LAYERS, D_MODEL, GDN_IN_DIM), D_MODEL ** -0.5),
        "gdn_conv_w": nrm(ks[5], (N_GDN_LAYERS, CONV_K, 2 * GDN_KEY_DIM + GDN_VAL_DIM), CONV_K ** -0.5),
        "gdn_a_log": jnp.log(jax.random.uniform(ks[6], (N_GDN_LAYERS, GDN_V_HEADS), jnp.float32, 1.0, 16.0)),
        "gdn_dt_bias": dt + jnp.log(-jnp.expm1(-dt)),
        "gdn_norm_w": 1.0 + nrm(ks[8], (N_GDN_LAYERS, GDN_HEAD_DIM), 0.02),
        "gdn_w_out": nrm(ks[9], (N_GDN_LAYERS, GDN_VAL_DIM, D_MODEL), GDN_VAL_DIM ** -0.5 * DEEPNORM_BETA),
        "ln1_g": 1.0 + nrm(ks[10], (DEPTH, D_MODEL), 0.02),
        "ln1_b": nrm(ks[11], (DEPTH, D_MODEL), 0.02),
        "router_w": nrm(ks[12], (D_MODEL, N_EXPERTS), D_MODEL ** -0.5),
        "router_b": nrm(ks[13], (N_EXPERTS,), 0.01),
        "expert_w_gate_up": nrm(ks[14], (DEPTH, N_EXPERTS, D_MODEL, 2 * D_EXPERT), D_MODEL ** -0.5),
        "expert_w_down": nrm(ks[15], (DEPTH, N_EXPERTS, D_EXPERT, D_MODEL), D_EXPERT ** -0.5 * DEEPNORM_BETA),
        "ple_w_gate": nrm(ks[16], (DEPTH, D_MODEL, D_MODEL), D_MODEL ** -0.5),
        "ple_w_proj": nrm(ks[17], (DEPTH, PLE_DIM, D_MODEL), PLE_DIM ** -0.5),
        "ln2_g": 1.0 + nrm(ks[18], (DEPTH, D_MODEL), 0.02),
        "ln2_b": nrm(ks[19], (DEPTH, D_MODEL), 0.02),
    }


def reference(x, p, sb_w_in, sb_w_out, gdn_w_in, gdn_conv_w, gdn_a_log, gdn_dt_bias, gdn_norm_w,
              gdn_w_out, ln1_g, ln1_b, router_w, router_b, expert_w_gate_up, expert_w_down,
              ple_w_gate, ple_w_proj, ln2_g, ln2_b):
    for i in range(DEPTH):
        j = i // N_MIXERS
        if i % N_MIXERS == 0:
            mix = stick_breaking_attention(x, sb_w_in[j], sb_w_out[j])
        else:
            mix = gated_deltanet(x, gdn_w_in[j], gdn_conv_w[j], gdn_a_log[j], gdn_dt_bias[j],
                                 gdn_norm_w[j], gdn_w_out[j])
        x = layer_norm(DEEPNORM_ALPHA * x + mix, ln1_g[i], ln1_b[i])
        moe = grouped_moe(x, router_w, router_b, expert_w_gate_up[i], expert_w_down[i])
        ple = jax.nn.sigmoid(x @ ple_w_gate[i]) * (p[i] @ ple_w_proj[i])
        x = layer_norm(DEEPNORM_ALPHA * x + moe + ple, ln2_g[i], ln2_b[i])
    return x
```

```python
import functools

import jax
import jax.numpy as jnp
from jax import lax
from jax.experimental import pallas as pl
from jax.experimental.pallas import tpu as pltpu

D_MODEL = 1024
DEPTH = 2
PLE_DIM = 256
SB_HEADS = 16
SB_HEAD_DIM = 64
GDN_QK_HEADS = 8
GDN_V_HEADS = 16
GDN_HEAD_DIM = 128
GDN_KEY_DIM = GDN_QK_HEADS * GDN_HEAD_DIM
GDN_VAL_DIM = GDN_V_HEADS * GDN_HEAD_DIM
GDN_QKV_DIM = 2 * GDN_KEY_DIM + GDN_VAL_DIM
CONV_K = 4
CHUNK = 64
N_EXPERTS = 16
N_GROUPS = 4
EXPERTS_PER_GROUP = 4
D_EXPERT = 512
N_BUCKETS = 24
LN_EPS = 1e-5
RMS_EPS = 1e-6
DEEPNORM_ALPHA = (2 * DEPTH) ** 0.25

LANES = 128
TAIL = LANES
VMEM_LIMIT = 48 * 1024 * 1024
SB_DEAD_LOG = -104.0

F32 = jnp.float32
BF16 = jnp.bfloat16


def _cparams(*sem):
    return pltpu.CompilerParams(dimension_semantics=sem, vmem_limit_bytes=VMEM_LIMIT)


def _dot(a, b):
    return jnp.dot(a, b, preferred_element_type=F32)


def _dot_nt(a, b):
    return lax.dot_general(a, b, (((1,), (1,)), ((), ())), preferred_element_type=F32)


def _split_dot(a, b_bf16, parts):
    acc = None
    rem = a
    for _ in range(parts):
        hi = rem.astype(BF16)
        term = _dot(hi, b_bf16)
        acc = term if acc is None else acc + term
        rem = rem - hi.astype(F32)
    return acc


def _linear_kernel(x_ref, w_ref, o_ref):
    o_ref[...] = _dot(x_ref[...].astype(BF16), w_ref[...]).astype(o_ref.dtype)


def _linear(x, w, out_dtype, tm, tn):
    m, k = x.shape
    n = w.shape[1]
    return pl.pallas_call(
        _linear_kernel,
        grid=(m // tm, n // tn),
        in_specs=[pl.BlockSpec((tm, k), lambda i, j: (i, 0)),
                  pl.BlockSpec((k, tn), lambda i, j: (0, j))],
        out_specs=pl.BlockSpec((tm, tn), lambda i, j: (i, j)),
        out_shape=jax.ShapeDtypeStruct((m, n), out_dtype),
        compiler_params=_cparams("parallel", "parallel"),
        name="linear",
    )(x, w)


def _sb_kernel(q_ref, k_ref, v_ref, o_ref, *, tq, tk, scale):
    t = q_ref.shape[0]
    rows = 2 * tq
    lane = lax.broadcasted_iota(jnp.int32, (tq, LANES), 1)
    first_head = lane < SB_HEAD_DIM
    jj = lax.broadcasted_iota(jnp.int32, (tk, 2 * tk), 0)
    ss = lax.broadcasted_iota(jnp.int32, (tk, 2 * tk), 1)
    cum_mat = jnp.where((jj > ss) | (ss >= tk), 1.0, 0.0).astype(BF16)
    row_in_tile = lax.broadcasted_iota(jnp.int32, (rows, tk), 0) % tq
    col_in_tile = lax.broadcasted_iota(jnp.int32, (rows, tk), 1)

    def q_tile(i, _):
        q0 = pl.multiple_of(i * tq, tq)
        q = q_ref[pl.ds(q0, tq), :]
        zero = jnp.zeros_like(q)
        qq = jnp.concatenate([jnp.where(first_head, q, zero), jnp.where(first_head, zero, q)], axis=0)
        row_pos = row_in_tile + q0

        def alive(c):
            j, carry, _ = c
            return jnp.logical_and(j >= 0, jnp.max(carry) >= SB_DEAD_LOG)

        def key_tile(c):
            j, carry, acc = c
            k0 = pl.multiple_of(j * tk, tk)
            kb = k_ref[pl.ds(k0, tk), :]
            vb = v_ref[pl.ds(k0, tk), :]
            z = _dot_nt(qq, kb) * scale
            causal = (col_in_tile + k0) < row_pos
            sp = jnp.maximum(z, 0.0) + jnp.log1p(jnp.exp(-jnp.abs(z)))
            log_1m_beta = jnp.where(causal, -sp, 0.0)
            cs = _split_dot(log_1m_beta, cum_mat, 2)
            log_rest = cs[:, :tk] + carry
            attn = jnp.where(causal, jnp.exp(z - sp + log_rest), 0.0)
            acc = acc + _dot(attn.astype(BF16), vb)
            return j - 1, carry + cs[:, tk:], acc

        init = (i * tq // tk + (tq // tk - 1), jnp.zeros((rows, tk), F32), jnp.zeros((rows, LANES), F32))
        _, _, acc = lax.while_loop(alive, key_tile, init)
        o_ref[pl.ds(q0, tq), :] = jnp.where(first_head, acc[:tq], acc[tq:]).astype(o_ref.dtype)
        return 0

    lax.fori_loop(0, t // tq, q_tile, 0)


def _sb_attention(qkv, bsz, t):
    tq = tk = 128
    ncol = D_MODEL // LANES
    kern = functools.partial(_sb_kernel, tq=tq, tk=tk, scale=SB_HEAD_DIM ** -0.5)
    return pl.pallas_call(
        kern,
        grid=(bsz, ncol),
        in_specs=[pl.BlockSpec((None, t, LANES), lambda b, h: (b, 0, h)),
                  pl.BlockSpec((None, t, LANES), lambda b, h: (b, 0, ncol + h)),
                  pl.BlockSpec((None, t, LANES), lambda b, h: (b, 0, 2 * ncol + h))],
        out_specs=pl.BlockSpec((None, t, LANES), lambda b, h: (b, 0, h)),
        out_shape=jax.ShapeDtypeStruct((bsz, t, D_MODEL), BF16),
        compiler_params=_cparams("parallel", "parallel"),
        name="sb_attention",
    )(qkv, qkv, qkv)


def _conv_kernel(x_ref, halo_ref, w_ref, o_ref, kt_ref, *, tt, tc):
    i = pl.program_id(1)
    j = pl.program_id(2)
    x = x_ref[...].astype(F32)
    halo = jnp.where(i > 0, halo_ref[...].astype(F32), 0.0)
    w = w_ref[...]
    sub8 = lax.broadcasted_iota(jnp.int32, (8, tc), 0)
    acc = x * w[CONV_K - 1:CONV_K, :]
    head = x[0:8] * w[CONV_K - 1:CONV_K, :]
    for s in range(1, CONV_K):
        wk = w[CONV_K - 1 - s:CONV_K - s, :]
        xs = pltpu.roll(x, s, 0)
        acc = acc + xs * wk
        first = jnp.where(sub8 < s, pltpu.roll(halo, s, 0), xs[0:8])
        head = head + first * wk
    y = jnp.concatenate([head, acc[8:]], axis=0)
    y = y * jax.nn.sigmoid(y)

    @pl.when(j >= 2 * GDN_KEY_DIM // tc)
    def _():
        o_ref[...] = y.astype(o_ref.dtype)

    @pl.when(j < 2 * GDN_KEY_DIM // tc)
    def _():
        parts = []
        for h in range(tc // GDN_HEAD_DIM):
            yh = y[:, h * GDN_HEAD_DIM:(h + 1) * GDN_HEAD_DIM]
            parts.append(yh * lax.rsqrt(jnp.sum(yh * yh, axis=1, keepdims=True) + RMS_EPS))
        yn = jnp.concatenate(parts, axis=1) if len(parts) > 1 else parts[0]
        o_ref[...] = yn.astype(o_ref.dtype)
        kt_ref[...] = yn.T.astype(kt_ref.dtype)


def _gdn_conv(proj, conv_w, bsz, t):
    tt, tc = 512, 512
    kern = functools.partial(_conv_kernel, tt=tt, tc=tc)
    nqk = 2 * GDN_KEY_DIM // tc
    return pl.pallas_call(
        kern,
        grid=(bsz, t // tt, GDN_QKV_DIM // tc),
        in_specs=[pl.BlockSpec((None, tt, tc), lambda b, i, j: (b, i, j)),
                  pl.BlockSpec((None, 8, tc), lambda b, i, j: (b, jnp.maximum(i * (tt // 8) - 1, 0), j)),
                  pl.BlockSpec((CONV_K, tc), lambda b, i, j: (0, j))],
        out_specs=[pl.BlockSpec((None, tt, tc), lambda b, i, j: (b, i, j)),
                   pl.BlockSpec((None, tc, tt), lambda b, i, j: (b, jnp.minimum(j, nqk - 1), i))],
        out_shape=[jax.ShapeDtypeStruct((bsz, t, GDN_QKV_DIM), BF16),
                   jax.ShapeDtypeStruct((bsz, 2 * GDN_KEY_DIM, t), BF16)],
        compiler_params=_cparams("parallel", "parallel", "arbitrary"),
        name="gdn_conv",
    )(proj, proj, conv_w)


def _gate_prep_kernel(ba_ref, alog_ref, dtb_ref, gb_ref, gbt_ref, *, tt):
    ba = ba_ref[...]
    lane = lax.broadcasted_iota(jnp.int32, (tt, LANES), 1)
    beta = jax.nn.sigmoid(ba)
    xa = ba + dtb_ref[...]
    softplus = jnp.maximum(xa, 0.0) + jnp.log1p(jnp.exp(-jnp.abs(xa)))
    g = -jnp.exp(alog_ref[...]) * softplus
    r = lax.broadcasted_iota(jnp.int32, (tt, tt), 0)
    c = lax.broadcasted_iota(jnp.int32, (tt, tt), 1)
    tri = jnp.where((c <= r) & (c // CHUNK == r // CHUNK), 1.0, 0.0).astype(BF16)
    is_g = (lane >= GDN_V_HEADS) & (lane < 2 * GDN_V_HEADS)
    g = jnp.where(is_g, g, 0.0)
    rem = g
    gc = jnp.zeros_like(g)
    for _ in range(3):
        hi = rem.astype(BF16)
        gc = gc + _dot(tri, hi)
        rem = rem - hi.astype(F32)
    out = jnp.where(lane < GDN_V_HEADS, beta, gc)
    gb_ref[...] = out
    gbt_ref[...] = out.T


def _gdn_gate_prep(ba, a_log_row, dt_bias_row, bsz, t):
    tt = 512
    kern = functools.partial(_gate_prep_kernel, tt=tt)
    return pl.pallas_call(
        kern,
        grid=(bsz, t // tt),
        in_specs=[pl.BlockSpec((None, tt, LANES), lambda b, i: (b, i, 0)),
                  pl.BlockSpec((1, LANES), lambda b, i: (0, 0)),
                  pl.BlockSpec((1, LANES), lambda b, i: (0, 0))],
        out_specs=[pl.BlockSpec((None, tt, LANES), lambda b, i: (b, i, 0)),
                   pl.BlockSpec((None, LANES, tt), lambda b, i: (b, 0, i))],
        out_shape=[jax.ShapeDtypeStruct((bsz, t, LANES), F32),
                   jax.ShapeDtypeStruct((bsz, LANES, t), F32)],
        compiler_params=_cparams("parallel", "parallel"),
        name="gdn_gate_prep",
    )(ba, a_log_row, dt_bias_row)


def _gdn_kernel(q_ref, k_ref, kt_ref, v_ref, z_ref, gb_ref, gbt_ref, nw_ref, o_ref, s_ref, *, tt):
    jh = pl.program_id(1)

    @pl.when(pl.program_id(2) == 0)
    def _():
        s_ref[...] = jnp.zeros_like(s_ref)

    blk = 2 * CHUNK
    lane = lax.broadcasted_iota(jnp.int32, (blk, LANES), 1)
    sub8 = lax.broadcasted_iota(jnp.int32, (8, blk), 0)
    ri = lax.broadcasted_iota(jnp.int32, (CHUNK, CHUNK), 0)
    ci = lax.broadcasted_iota(jnp.int32, (CHUNK, CHUNK), 1)
    lower_incl = ri >= ci
    strict = ri > ci
    qscale = GDN_HEAD_DIM ** -0.5
    nw = nw_ref[...]
    grow_base = pl.multiple_of(GDN_V_HEADS + (jh // 4) * 8, 8)

    def block(m, _):
        r0 = pl.multiple_of(m * blk, blk)
        qb = q_ref[pl.ds(r0, blk), :].astype(F32) * qscale
        kb = k_ref[pl.ds(r0, blk), :].astype(F32)
        ktb = kt_ref[:, pl.ds(r0, blk)]
        gbb = gb_ref[pl.ds(r0, blk), :]
        rows8 = gbt_ref[pl.ds(grow_base, 8), pl.ds(r0, blk)]
        for s in range(2):
            h = 2 * jh + s
            beta_col = jnp.sum(jnp.where(lane == h, gbb, 0.0), axis=1, keepdims=True)
            gc_col = jnp.sum(jnp.where(lane == GDN_V_HEADS + h, gbb, 0.0), axis=1, keepdims=True)
            gc_row = jnp.sum(jnp.where(sub8 == h % 8, rows8, 0.0), axis=0, keepdims=True)
            for c in range(2):
                lo, hi = c * CHUNK, (c + 1) * CHUNK
                q_c, k_c = qb[lo:hi], kb[lo:hi]
                kt_c = ktb[:, lo:hi]
                k_bf = k_c.astype(BF16)
                kk = _dot(k_bf, kt_c)
                qk = _dot(q_c.astype(BF16), kt_c)
                rr = pl.ds(r0 + lo, CHUNK)
                v_c = v_ref[rr, s * LANES:(s + 1) * LANES].astype(F32)
                bcol, gcol, grow = beta_col[lo:hi], gc_col[lo:hi], gc_row[:, lo:hi]
                decay = jnp.where(lower_incl, jnp.exp(jnp.minimum(gcol - grow, 0.0)), 0.0)
                a = jnp.where(strict, bcol * kk * decay, 0.0)
                attn = qk * decay
                eg = jnp.exp(gcol)
                rhs = jnp.concatenate([v_c * bcol, k_c * (bcol * eg)], axis=1)
                p = a.astype(BF16)
                rhs = rhs - _dot(p, rhs.astype(BF16))
                for _ in range(5):
                    pf = _dot(p, p)
                    p = pf.astype(BF16)
                    rhs = rhs + _dot(p, rhs.astype(BF16))
                u, w = rhs[:, :LANES], rhs[:, LANES:]
                st = s_ref[s]
                st_bf = st.astype(BF16)
                v_new = u - _dot(w.astype(BF16), st_bf)
                v_new_bf = v_new.astype(BF16)
                o_c = _dot((q_c * eg).astype(BF16), st_bf) + _dot(attn.astype(BF16), v_new_bf)
                g_last = gcol[CHUNK - 1:CHUNK, :]
                kt_dec = (kt_c.astype(F32) * jnp.exp(g_last - grow)).astype(BF16)
                s_ref[s] = st * jnp.exp(g_last) + _dot(kt_dec, v_new_bf)
                o_n = o_c * lax.rsqrt(jnp.mean(o_c * o_c, axis=1, keepdims=True) + RMS_EPS) * nw
                zz = z_ref[rr, s * LANES:(s + 1) * LANES].astype(F32)
                o_ref[rr, s * LANES:(s + 1) * LANES] = (o_n * (zz * jax.nn.sigmoid(zz))).astype(o_ref.dtype)
        return 0

    lax.fori_loop(0, tt // blk, block, 0)


def _gdn_recurrence(qkv, kt, proj, gb, gbt, norm_w_row, bsz, t):
    tt = 1024
    kern = functools.partial(_gdn_kernel, tt=tt)
    nk = GDN_KEY_DIM // LANES
    return pl.pallas_call(
        kern,
        grid=(bsz, GDN_QK_HEADS, t // tt),
        in_specs=[pl.BlockSpec((None, tt, LANES), lambda b, j, i: (b, i, j)),
                  pl.BlockSpec((None, tt, LANES), lambda b, j, i: (b, i, nk + j)),
                  pl.BlockSpec((None, LANES, tt), lambda b, j, i: (b, nk + j, i)),
                  pl.BlockSpec((None, tt, 2 * LANES), lambda b, j, i: (b, i, nk + j)),
                  pl.BlockSpec((None, tt, 2 * LANES), lambda b, j, i: (b, i, 2 * nk + j)),
                  pl.BlockSpec((None, tt, LANES), lambda b, j, i: (b, i, 0)),
                  pl.BlockSpec((None, LANES, tt), lambda b, j, i: (b, 0, i)),
                  pl.BlockSpec((1, LANES), lambda b, j, i: (0, 0))],
        out_specs=pl.BlockSpec((None, tt, 2 * LANES), lambda b, j, i: (b, i, j)),
        out_shape=jax.ShapeDtypeStruct((bsz, t, GDN_VAL_DIM), BF16),
        scratch_shapes=[pltpu.VMEM((2, GDN_HEAD_DIM, GDN_HEAD_DIM), F32)],
        compiler_params=_cparams("parallel", "parallel", "arbitrary"),
        name="gdn_recurrence",
    )(qkv, qkv, kt, qkv, proj, gb, gbt, norm_w_row)


def _layer_norm(r, g, b):
    mu = jnp.mean(r, axis=1, keepdims=True)
    d = r - mu
    var = jnp.mean(d * d, axis=1, keepdims=True)
    return d * lax.rsqrt(var + LN_EPS) * g + b


def _post_mix_kernel(o_ref, w_ref, x_ref, g_ref, b_ref, rw_ref, rb_ref, hx_ref, cnt_ref, *, tm):
    @pl.when(pl.program_id(0) == 0)
    def _():
        cnt_ref[...] = jnp.zeros_like(cnt_ref)

    mix = _dot(o_ref[...], w_ref[...])
    h = _layer_norm(DEEPNORM_ALPHA * x_ref[...] + mix, g_ref[...], b_ref[...])
    hx_ref[:, :D_MODEL] = h

    logits = jnp.dot(h, rw_ref[...], precision=lax.Precision.HIGHEST, preferred_element_type=F32)
    scores = jax.nn.sigmoid(logits)
    lane = lax.broadcasted_iota(jnp.int32, (tm, LANES), 1)
    neg = -jnp.inf
    sel = jnp.where(lane < N_EXPERTS, scores + rb_ref[...], neg)
    grp = lane // EXPERTS_PER_GROUP

    def top2(x):
        m1 = jnp.max(x, axis=1, keepdims=True)
        i1 = jnp.min(jnp.where(x == m1, lane, LANES), axis=1, keepdims=True)
        x2 = jnp.where(lane == i1, neg, x)
        m2 = jnp.max(x2, axis=1, keepdims=True)
        i2 = jnp.min(jnp.where(x2 == m2, lane, LANES), axis=1, keepdims=True)
        return m1, i1, m2, i2

    best = jnp.zeros((tm, 1), jnp.int32)
    best_score = None
    for gidx in range(N_GROUPS):
        m1, _, m2, _ = top2(jnp.where(grp == gidx, sel, neg))
        gs = m1 + m2
        if best_score is None:
            best_score = gs
        else:
            upd = gs > best_score
            best = jnp.where(upd, gidx, best)
            best_score = jnp.where(upd, gs, best_score)
    _, i1, _, i2 = top2(jnp.where(grp == best, sel, neg))
    s1 = jnp.sum(jnp.where(lane == i1, scores, 0.0), axis=1, keepdims=True)
    s2 = jnp.sum(jnp.where(lane == i2, scores, 0.0), axis=1, keepdims=True)
    denom = s1 + s2
    w1, w2 = s1 / denom, s2 / denom
    first_lower = i1 < i2
    w_lo = jnp.where(first_lower, w1, w2)
    w_hi = jnp.where(first_lower, w2, w1)
    a = jnp.minimum(i1, i2) % EXPERTS_PER_GROUP
    bb = jnp.maximum(i1, i2) % EXPERTS_PER_GROUP
    pair = jnp.where(a == 0, bb - 1, jnp.where(a == 1, bb + 1, 5))
    bucket = best * 6 + pair

    onehot = jnp.where(lane == bucket, 1.0, 0.0)
    r = lax.broadcasted_iota(jnp.int32, (tm, tm), 0)
    c = lax.broadcasted_iota(jnp.int32, (tm, tm), 1)
    before = jnp.where(c < r, 1.0, 0.0).astype(BF16)
    prefix = _dot(before, onehot.astype(BF16)) + cnt_ref[...]
    rank = jnp.sum(jnp.where(lane == bucket, prefix, 0.0), axis=1, keepdims=True)
    cnt_ref[...] += jnp.sum(onehot, axis=0, keepdims=True)

    tail = jnp.where(lane == 0, w_lo, jnp.where(lane == 1, w_hi, jnp.where(
        lane == 2, bucket.astype(F32), jnp.where(lane == 3, rank, 0.0))))
    hx_ref[:, D_MODEL:] = tail


def _post_mix(o, w_out, x, ln_g, ln_b, router_w_pad, router_b_pad):
    n, ko = o.shape
    tm = 512
    kern = functools.partial(_post_mix_kernel, tm=tm)
    row = lambda i: (0, 0)
    return pl.pallas_call(
        kern,
        grid=(n // tm,),
        in_specs=[pl.BlockSpec((tm, ko), lambda i: (i, 0)),
                  pl.BlockSpec((ko, D_MODEL), row),
                  pl.BlockSpec((tm, D_MODEL), lambda i: (i, 0)),
                  pl.BlockSpec((1, D_MODEL), row),
                  pl.BlockSpec((1, D_MODEL), row),
                  pl.BlockSpec((D_MODEL, LANES), row),
                  pl.BlockSpec((1, LANES), row)],
        out_specs=[pl.BlockSpec((tm, D_MODEL + TAIL), lambda i: (i, 0)),
                   pl.BlockSpec((1, LANES), row)],
        out_shape=[jax.ShapeDtypeStruct((n, D_MODEL + TAIL), F32),
                   jax.ShapeDtypeStruct((1, LANES), F32)],
        compiler_params=_cparams("arbitrary"),
        name="post_mix",
    )(o, w_out, x, ln_g, ln_b, router_w_pad, router_b_pad)


PERM_CHUNK = 1024


def _row_copy(src_ref, dst_ref, sem, s, d):
    return pltpu.make_async_copy(src_ref.at[pl.ds(s, 1)], dst_ref.at[pl.ds(d, 1)], sem)


def _scatter_rows_kernel(idx_ref, src_ref, init_ref, out_ref, sem):
    del init_ref
    base = pl.program_id(0) * PERM_CHUNK

    def issue(i, _):
        _row_copy(src_ref, out_ref, sem, base + i, idx_ref[i]).start()
        return 0

    def drain(i, _):
        _row_copy(src_ref, out_ref, sem, 0, 0).wait()
        return 0

    lax.fori_loop(0, PERM_CHUNK, issue, 0)
    lax.fori_loop(0, PERM_CHUNK, drain, 0)


def _scatter_rows(src, dest, n_out):
    n, w = src.shape
    return pl.pallas_call(
        _scatter_rows_kernel,
        grid=(n // PERM_CHUNK,),
        in_specs=[pl.BlockSpec((PERM_CHUNK,), lambda i: (i,), memory_space=pltpu.SMEM),
                  pl.BlockSpec(memory_space=pl.ANY),
                  pl.BlockSpec(memory_space=pl.ANY)],
        out_specs=pl.BlockSpec(memory_space=pl.ANY),
        out_shape=jax.ShapeDtypeStruct((n_out, w), src.dtype),
        scratch_shapes=[pltpu.SemaphoreType.DMA(())],
        input_output_aliases={2: 0},
        compiler_params=pltpu.CompilerParams(dimension_semantics=("arbitrary",), has_side_effects=True),
        name="scatter_rows",
    )(dest, src, jnp.zeros((n_out, w), src.dtype))


def _gather_rows_kernel(idx_ref, src_ref, out_ref, sem):
    base = pl.program_id(0) * PERM_CHUNK

    def issue(i, _):
        _row_copy(src_ref, out_ref, sem, idx_ref[i], base + i).start()
        return 0

    def drain(i, _):
        _row_copy(src_ref, out_ref, sem, 0, 0).wait()
        return 0

    lax.fori_loop(0, PERM_CHUNK, issue, 0)
    lax.fori_loop(0, PERM_CHUNK, drain, 0)


def _gather_rows(src, idx):
    n = idx.shape[0]
    w = src.shape[1]
    return pl.pallas_call(
        _gather_rows_kernel,
        grid=(n // PERM_CHUNK,),
        in_specs=[pl.BlockSpec((PERM_CHUNK,), lambda i: (i,), memory_space=pltpu.SMEM),
                  pl.BlockSpec(memory_space=pl.ANY)],
        out_specs=pl.BlockSpec(memory_space=pl.ANY),
        out_shape=jax.ShapeDtypeStruct((n, w), src.dtype),
        scratch_shapes=[pltpu.SemaphoreType.DMA(())],
        compiler_params=pltpu.CompilerParams(dimension_semantics=("arbitrary",), has_side_effects=True),
        name="gather_rows",
    )(idx, src)


MOE_TILE = 256


def _moe_kernel(src_ref, ea_ref, eb_ref, nused_ref, xs_ref, gua_ref, gub_ref, da_ref, db_ref, y_ref):
    del src_ref, ea_ref, eb_ref

    @pl.when(pl.program_id(0) >= nused_ref[0])
    def _():
        y_ref[...] = jnp.zeros_like(y_ref)

    @pl.when(pl.program_id(0) < nused_ref[0])
    def _():
        x = xs_ref[:, :D_MODEL].astype(BF16)
        tail = xs_ref[:, D_MODEL:]
        lane = lax.broadcasted_iota(jnp.int32, tail.shape, 1)
        w_lo = jnp.sum(jnp.where(lane == 0, tail, 0.0), axis=1, keepdims=True)
        w_hi = jnp.sum(jnp.where(lane == 1, tail, 0.0), axis=1, keepdims=True)

        def expert(gu_ref, d_ref):
            gu = _dot(x, gu_ref[...])
            g_in, u_in = gu[:, :D_EXPERT], gu[:, D_EXPERT:]
            act = (g_in * jax.nn.sigmoid(g_in)) * u_in
            return _dot(act.astype(BF16), d_ref[...])

        y_ref[...] = w_lo * expert(gua_ref, da_ref) + w_hi * expert(gub_ref, db_ref)


def _moe_ffn(xs, tile_src, tile_ea, tile_eb, n_used, w_gate_up, w_down):
    n_pad = xs.shape[0]
    n_tiles = n_pad // MOE_TILE
    grid_spec = pltpu.PrefetchScalarGridSpec(
        num_scalar_prefetch=4,
        grid=(n_tiles,),
        in_specs=[pl.BlockSpec((MOE_TILE, D_MODEL + TAIL), lambda t, src, ea, eb, nu: (src[t], 0)),
                  pl.BlockSpec((None, D_MODEL, 2 * D_EXPERT), lambda t, src, ea, eb, nu: (ea[t], 0, 0)),
                  pl.BlockSpec((None, D_MODEL, 2 * D_EXPERT), lambda t, src, ea, eb, nu: (eb[t], 0, 0)),
                  pl.BlockSpec((None, D_EXPERT, D_MODEL), lambda t, src, ea, eb, nu: (ea[t], 0, 0)),
                  pl.BlockSpec((None, D_EXPERT, D_MODEL), lambda t, src, ea, eb, nu: (eb[t], 0, 0))],
        out_specs=pl.BlockSpec((MOE_TILE, D_MODEL), lambda t, src, ea, eb, nu: (t, 0)),
    )
    return pl.pallas_call(
        _moe_kernel,
        grid_spec=grid_spec,
        out_shape=jax.ShapeDtypeStruct((n_pad, D_MODEL), F32),
        compiler_params=_cparams("arbitrary"),
        name="moe_ffn",
    )(tile_src, tile_ea, tile_eb, n_used, xs, w_gate_up, w_gate_up, w_down, w_down)


def _routing_plan(hx, counts, n):
    n_tiles = n // MOE_TILE + N_BUCKETS
    cnt = counts[0, :N_BUCKETS].astype(jnp.int32)
    tiles_per = (cnt + MOE_TILE - 1) // MOE_TILE
    tile_end = jnp.cumsum(tiles_per)
    tile_start = tile_end - tiles_per
    bucket = hx[:, D_MODEL + 2].astype(jnp.int32)
    rank = hx[:, D_MODEL + 3].astype(jnp.int32)
    dest = tile_start[bucket] * MOE_TILE + rank
    n_used = tile_end[-1]
    tile_id = jnp.minimum(jnp.arange(n_tiles, dtype=jnp.int32), n_used - 1)
    tile_bucket = jnp.sum(tile_id[:, None] >= tile_end[None, :], axis=1).astype(jnp.int32)
    group = tile_bucket // 6
    pair = tile_bucket % 6
    lo = jnp.where(pair < 3, 0, jnp.where(pair < 5, 1, 2))
    hi = jnp.where(pair < 3, pair + 1, jnp.where(pair < 5, pair - 1, 3))
    return dest, tile_id, group * 4 + lo, group * 4 + hi, n_used.reshape(1).astype(jnp.int32), n_tiles * MOE_TILE


def _final_kernel(hx_ref, moe_ref, p_ref, wg_ref, wp_ref, g_ref, b_ref, o_ref):
    h = hx_ref[...]
    gate = jax.nn.sigmoid(_dot(h.astype(BF16), wg_ref[...]))
    ple = gate * _dot(p_ref[...].astype(BF16), wp_ref[...])
    o_ref[...] = _layer_norm(DEEPNORM_ALPHA * h + moe_ref[...] + ple, g_ref[...], b_ref[...])


def _final(hx, moe, p, w_gate, w_proj, ln_g, ln_b):
    n = moe.shape[0]
    tm = 512
    row = lambda i: (0, 0)
    return pl.pallas_call(
        _final_kernel,
        grid=(n // tm,),
        in_specs=[pl.BlockSpec((tm, D_MODEL), lambda i: (i, 0)),
                  pl.BlockSpec((tm, D_MODEL), lambda i: (i, 0)),
                  pl.BlockSpec((tm, PLE_DIM), lambda i: (i, 0)),
                  pl.BlockSpec((D_MODEL, D_MODEL), row),
                  pl.BlockSpec((PLE_DIM, D_MODEL), row),
                  pl.BlockSpec((1, D_MODEL), row),
                  pl.BlockSpec((1, D_MODEL), row)],
        out_specs=pl.BlockSpec((tm, D_MODEL), lambda i: (i, 0)),
        out_shape=jax.ShapeDtypeStruct((n, D_MODEL), F32),
        compiler_params=_cparams("parallel"),
        name="final",
    )(hx, moe, p, w_gate, w_proj, ln_g, ln_b)


def _pad_lanes(a, width=LANES):
    return jnp.pad(a, [(0, 0)] * (a.ndim - 1) + [(0, width - a.shape[-1])])


def _moe_and_norm(o, w_out, x, p_i, ln1_g, ln1_b, rw_pad, rb_pad, w_gate_up, w_down, ple_w_gate, ple_w_proj,
                  ln2_g, ln2_b):
    n = x.shape[0]
    hx, counts = _post_mix(o, w_out.astype(BF16), x, ln1_g[None], ln1_b[None], rw_pad, rb_pad)
    dest, tile_src, tile_ea, tile_eb, n_used, n_pad = _routing_plan(hx, counts, n)
    xs = _scatter_rows(hx, dest, n_pad)
    ys = _moe_ffn(xs, tile_src, tile_ea, tile_eb, n_used, w_gate_up.astype(BF16), w_down.astype(BF16))
    moe = _gather_rows(ys, dest)
    return _final(hx, moe, p_i, ple_w_gate.astype(BF16), ple_w_proj.astype(BF16), ln2_g[None], ln2_b[None])


def kernel(x, p, sb_w_in, sb_w_out, gdn_w_in, gdn_conv_w, gdn_a_log, gdn_dt_bias, gdn_norm_w, gdn_w_out,
           ln1_g, ln1_b, router_w, router_b, expert_w_gate_up, expert_w_down, ple_w_gate, ple_w_proj,
           ln2_g, ln2_b):
    bsz, t, d = x.shape
    n = bsz * t
    x = x.reshape(n, d)
    p = p.reshape(DEPTH, n, PLE_DIM)
    rw_pad = _pad_lanes(router_w)
    rb_pad = _pad_lanes(router_b[None])

    qkv = _linear(x, sb_w_in[0].astype(BF16), BF16, 1024, 1024)
    o = _sb_attention(qkv.reshape(bsz, t, 3 * d), bsz, t).reshape(n, d)
    x = _moe_and_norm(o, sb_w_out[0], x, p[0], ln1_g[0], ln1_b[0], rw_pad, rb_pad, expert_w_gate_up[0],
                      expert_w_down[0], ple_w_gate[0], ple_w_proj[0], ln2_g[0], ln2_b[0])

    w_in = gdn_w_in[0]
    split = GDN_QKV_DIM + GDN_VAL_DIM
    proj = _linear(x, w_in[:, :split].astype(BF16), BF16, 1024, 1024)
    ba = _linear(x, _pad_lanes(w_in[:, split:]).astype(BF16), F32, 1024, LANES)
    proj = proj.reshape(bsz, t, split)
    qkv_c, kt = _gdn_conv(proj, gdn_conv_w[0], bsz, t)
    head_pad = lambda a: jnp.pad(a, (GDN_V_HEADS, LANES - 2 * GDN_V_HEADS))[None]
    gb, gbt = _gdn_gate_prep(ba.reshape(bsz, t, LANES), head_pad(gdn_a_log[0]), head_pad(gdn_dt_bias[0]), bsz, t)
    o = _gdn_recurrence(qkv_c, kt, proj, gb, gbt, gdn_norm_w[0][None], bsz, t).reshape(n, GDN_VAL_DIM)
    x = _moe_and_norm(o, gdn_w_out[0], x, p[1], ln1_g[1], ln1_b[1], rw_pad, rb_pad, expert_w_gate_up[1],
                      expert_w_down[1], ple_w_gate[1], ple_w_proj[1], ln2_g[1], ln2_b[1])
    return x.reshape(bsz, t, d)
```

```python
import functools

import jax
import jax.numpy as jnp
from jax import lax
from jax.experimental import pallas as pl
from jax.experimental.pallas import tpu as pltpu

D_MODEL = 1024
DEPTH = 2
PLE_DIM = 256
SB_HEADS = 16
SB_HEAD_DIM = 64
GDN_QK_HEADS = 8
GDN_V_HEADS = 16
GDN_HEAD_DIM = 128
GDN_KEY_DIM = GDN_QK_HEADS * GDN_HEAD_DIM
GDN_VAL_DIM = GDN_V_HEADS * GDN_HEAD_DIM
GDN_QKV_DIM = 2 * GDN_KEY_DIM + GDN_VAL_DIM
CONV_K = 4
CHUNK = 64
N_EXPERTS = 16
N_GROUPS = 4
EXPERTS_PER_GROUP = 4
D_EXPERT = 512
N_BUCKETS = 24
LN_EPS = 1e-5
RMS_EPS = 1e-6
DEEPNORM_ALPHA = (2 * DEPTH) ** 0.25

LANES = 128
TAIL = LANES
VMEM_LIMIT = 48 * 1024 * 1024
SB_DEAD_LOG = -104.0

F32 = jnp.float32
BF16 = jnp.bfloat16


def _cparams(*sem):
    return pltpu.CompilerParams(dimension_semantics=sem, vmem_limit_bytes=VMEM_LIMIT)


def _dot(a, b):
    return jnp.dot(a, b, preferred_element_type=F32)


def _dot_nt(a, b):
    return lax.dot_general(a, b, (((1,), (1,)), ((), ())), preferred_element_type=F32)


def _split_dot(a, b_bf16, parts):
    acc = None
    rem = a
    for _ in range(parts):
        hi = rem.astype(BF16)
        term = _dot(hi, b_bf16)
        acc = term if acc is None else acc + term
        rem = rem - hi.astype(F32)
    return acc


def _linear_kernel(x_ref, w_ref, o_ref):
    o_ref[...] = _dot(x_ref[...].astype(BF16), w_ref[...]).astype(o_ref.dtype)


def _linear(x, w, out_dtype, tm, tn):
    m, k = x.shape
    n = w.shape[1]
    return pl.pallas_call(
        _linear_kernel,
        grid=(m // tm, n // tn),
        in_specs=[pl.BlockSpec((tm, k), lambda i, j: (i, 0)),
                  pl.BlockSpec((k, tn), lambda i, j: (0, j))],
        out_specs=pl.BlockSpec((tm, tn), lambda i, j: (i, j)),
        out_shape=jax.ShapeDtypeStruct((m, n), out_dtype),
        compiler_params=_cparams("parallel", "parallel"),
        name="linear",
    )(x, w)


SB_TQ = 128
SB_TK = 128
SB_GROUP = 4
SB_PIECE = 64


def _sb_kernel(q_ref, k_ref, v_ref, o_ref, carry_ref, acc_ref, *, scale):
    t = q_ref.shape[0]
    tq, tk, grp = SB_TQ, SB_TK, SB_GROUP
    rows = 2 * tq
    nrow = grp * rows
    npiece = nrow // SB_PIECE
    lane = lax.broadcasted_iota(jnp.int32, (tq, LANES), 1)
    first_head = lane < SB_HEAD_DIM
    jj = lax.broadcasted_iota(jnp.int32, (tk, 2 * tk), 0)
    ss = lax.broadcasted_iota(jnp.int32, (tk, 2 * tk), 1)
    cum_mat = jnp.where((jj > ss) | (ss >= tk), 1.0, 0.0).astype(BF16)
    r = lax.broadcasted_iota(jnp.int32, (nrow, tk), 0)
    row_in_tile = r % tq
    tile_of_row = r // rows
    col = lax.broadcasted_iota(jnp.int32, (nrow, tk), 1)
    piece = lambda a, p: a[p * SB_PIECE:(p + 1) * SB_PIECE]

    def group(gi, _):
        first_tile = gi * grp
        qs = []
        for a in range(grp):
            q = q_ref[pl.ds(pl.multiple_of((first_tile + a) * tq, tq), tq), :] * scale
            zero = jnp.zeros_like(q)
            qs += [jnp.where(first_head, q, zero), jnp.where(first_head, zero, q)]
        qq = jnp.concatenate(qs, axis=0).reshape(grp, rows, LANES)
        carry_ref[...] = jnp.zeros_like(carry_ref)
        acc_ref[...] = jnp.zeros_like(acc_ref)

        def alive(c):
            d, top = c
            return jnp.logical_and(d < first_tile + grp, top >= SB_DEAD_LOG)

        def step(c):
            d, _ = c
            ks, vs = [], []
            for a in range(grp):
                k0 = pl.multiple_of(jnp.maximum(first_tile + a - d, 0) * tk, tk)
                ks.append(k_ref[pl.ds(k0, tk), :])
                vs.append(v_ref[pl.ds(k0, tk), :])
            z = jnp.einsum("gqd,gkd->gqk", qq, jnp.stack(ks), preferred_element_type=F32).reshape(nrow, tk)
            limit = row_in_tile + jnp.where(d > 0, tk, 0)
            causal = (col < limit) & (tile_of_row + (first_tile - d) >= 0)
            his, los, lps = [], [], []
            for p in range(npiece):
                zp = piece(z, p)
                mp = piece(causal, p)
                sp = jnp.maximum(zp, 0.0) + jnp.log(1.0 + jnp.exp(-jnp.abs(zp)))
                lm = jnp.where(mp, -sp, 0.0)
                hi = lm.astype(BF16)
                his.append(hi)
                los.append((lm - hi.astype(F32)).astype(BF16))
                lps.append(jnp.where(mp, zp - sp, -1e30))
            cs = _dot(jnp.concatenate(his, axis=0), cum_mat) + _dot(jnp.concatenate(los, axis=0), cum_mat)
            attn, top = [], None
            for p in range(npiece):
                rr = pl.ds(p * SB_PIECE, SB_PIECE)
                csp = piece(cs, p)
                carry = carry_ref[rr, :]
                attn.append(jnp.exp(lps[p] + csp[:, :tk] + carry).astype(BF16))
                carry = carry + csp[:, tk:]
                carry_ref[rr, :] = carry
                top = carry if top is None else jnp.maximum(top, carry)
            av = jnp.einsum("gqk,gkd->gqd", jnp.concatenate(attn, axis=0).reshape(grp, rows, tk), jnp.stack(vs),
                            preferred_element_type=F32)
            acc_ref[...] += av.reshape(nrow, LANES)
            return d + 1, jnp.max(top)

        lax.while_loop(alive, step, (jnp.int32(0), jnp.float32(0.0)))
        for a in range(grp):
            acc = acc_ref[a * rows:(a + 1) * rows, :]
            o_ref[pl.ds(pl.multiple_of((first_tile + a) * tq, tq), tq), :] = jnp.where(
                first_head, acc[:tq], acc[tq:]).astype(o_ref.dtype)
        return 0

    lax.fori_loop(0, t // (tq * grp), group, 0)


def _sb_attention(qkv, bsz, t):
    ncol = D_MODEL // LANES
    nrow = SB_GROUP * 2 * SB_TQ
    kern = functools.partial(_sb_kernel, scale=SB_HEAD_DIM ** -0.5)
    return pl.pallas_call(
        kern,
        grid=(bsz, ncol),
        in_specs=[pl.BlockSpec((None, t, LANES), lambda b, h: (b, 0, h)),
                  pl.BlockSpec((None, t, LANES), lambda b, h: (b, 0, ncol + h)),
                  pl.BlockSpec((None, t, LANES), lambda b, h: (b, 0, 2 * ncol + h))],
        out_specs=pl.BlockSpec((None, t, LANES), lambda b, h: (b, 0, h)),
        out_shape=jax.ShapeDtypeStruct((bsz, t, D_MODEL), BF16),
        scratch_shapes=[pltpu.VMEM((nrow, SB_TK), F32), pltpu.VMEM((nrow, LANES), F32)],
        compiler_params=_cparams("parallel", "parallel"),
        name="sb_attention",
    )(qkv, qkv, qkv)


def _conv_kernel(x_ref, halo_ref, w_ref, o_ref, kt_ref, *, tt, tc):
    i = pl.program_id(1)
    j = pl.program_id(2)
    x = x_ref[...].astype(F32)
    halo = jnp.where(i > 0, halo_ref[...].astype(F32), 0.0)
    w = w_ref[...]
    sub8 = lax.broadcasted_iota(jnp.int32, (8, tc), 0)
    acc = x * w[CONV_K - 1:CONV_K, :]
    head = x[0:8] * w[CONV_K - 1:CONV_K, :]
    for s in range(1, CONV_K):
        wk = w[CONV_K - 1 - s:CONV_K - s, :]
        xs = pltpu.roll(x, s, 0)
        acc = acc + xs * wk
        first = jnp.where(sub8 < s, pltpu.roll(halo, s, 0), xs[0:8])
        head = head + first * wk
    y = jnp.concatenate([head, acc[8:]], axis=0)
    y = y * jax.nn.sigmoid(y)

    @pl.when(j >= 2 * GDN_KEY_DIM // tc)
    def _():
        o_ref[...] = y.astype(o_ref.dtype)

    @pl.when(j < 2 * GDN_KEY_DIM // tc)
    def _():
        parts = []
        for h in range(tc // GDN_HEAD_DIM):
            yh = y[:, h * GDN_HEAD_DIM:(h + 1) * GDN_HEAD_DIM]
            parts.append(yh * lax.rsqrt(jnp.sum(yh * yh, axis=1, keepdims=True) + RMS_EPS))
        yn = jnp.concatenate(parts, axis=1) if len(parts) > 1 else parts[0]
        o_ref[...] = yn.astype(o_ref.dtype)
        kt_ref[...] = yn.T.astype(kt_ref.dtype)


def _gdn_conv(proj, conv_w, bsz, t):
    tt, tc = 512, 512
    kern = functools.partial(_conv_kernel, tt=tt, tc=tc)
    nqk = 2 * GDN_KEY_DIM // tc
    return pl.pallas_call(
        kern,
        grid=(bsz, t // tt, GDN_QKV_DIM // tc),
        in_specs=[pl.BlockSpec((None, tt, tc), lambda b, i, j: (b, i, j)),
                  pl.BlockSpec((None, 8, tc), lambda b, i, j: (b, jnp.maximum(i * (tt // 8) - 1, 0), j)),
                  pl.BlockSpec((CONV_K, tc), lambda b, i, j: (0, j))],
        out_specs=[pl.BlockSpec((None, tt, tc), lambda b, i, j: (b, i, j)),
                   pl.BlockSpec((None, tc, tt), lambda b, i, j: (b, jnp.minimum(j, nqk - 1), i))],
        out_shape=[jax.ShapeDtypeStruct((bsz, t, GDN_QKV_DIM), BF16),
                   jax.ShapeDtypeStruct((bsz, 2 * GDN_KEY_DIM, t), BF16)],
        compiler_params=_cparams("parallel", "parallel", "arbitrary"),
        name="gdn_conv",
    )(proj, proj, conv_w)


def _gate_prep_kernel(ba_ref, alog_ref, dtb_ref, gb_ref, gbt_ref, *, tt):
    ba = ba_ref[...]
    lane = lax.broadcasted_iota(jnp.int32, (tt, LANES), 1)
    beta = jax.nn.sigmoid(ba)
    xa = ba + dtb_ref[...]
    softplus = jnp.maximum(xa, 0.0) + jnp.log1p(jnp.exp(-jnp.abs(xa)))
    g = -jnp.exp(alog_ref[...]) * softplus
    r = lax.broadcasted_iota(jnp.int32, (tt, tt), 0)
    c = lax.broadcasted_iota(jnp.int32, (tt, tt), 1)
    tri = jnp.where((c <= r) & (c // CHUNK == r // CHUNK), 1.0, 0.0).astype(BF16)
    is_g = (lane >= GDN_V_HEADS) & (lane < 2 * GDN_V_HEADS)
    g = jnp.where(is_g, g, 0.0)
    rem = g
    gc = jnp.zeros_like(g)
    for _ in range(3):
        hi = rem.astype(BF16)
        gc = gc + _dot(tri, hi)
        rem = rem - hi.astype(F32)
    out = jnp.where(lane < GDN_V_HEADS, beta, gc)
    gb_ref[...] = out
    gbt_ref[...] = out.T


def _gdn_gate_prep(ba, a_log_row, dt_bias_row, bsz, t):
    tt = 512
    kern = functools.partial(_gate_prep_kernel, tt=tt)
    return pl.pallas_call(
        kern,
        grid=(bsz, t // tt),
        in_specs=[pl.BlockSpec((None, tt, LANES), lambda b, i: (b, i, 0)),
                  pl.BlockSpec((1, LANES), lambda b, i: (0, 0)),
                  pl.BlockSpec((1, LANES), lambda b, i: (0, 0))],
        out_specs=[pl.BlockSpec((None, tt, LANES), lambda b, i: (b, i, 0)),
                   pl.BlockSpec((None, LANES, tt), lambda b, i: (b, 0, i))],
        out_shape=[jax.ShapeDtypeStruct((bsz, t, LANES), F32),
                   jax.ShapeDtypeStruct((bsz, LANES, t), F32)],
        compiler_params=_cparams("parallel", "parallel"),
        name="gdn_gate_prep",
    )(ba, a_log_row, dt_bias_row)


GDN_TT = 512
GDN_HG = 8


def _bmm(a, b):
    return jnp.einsum("cik,ckj->cij", a, b, preferred_element_type=F32)


def _gdn_kernel(q_ref, k_ref, kt_ref, v_ref, z_ref, gb_ref, gbt_ref, nw_ref, o_ref,
                s_ref, u_ref, w_ref, at_ref, qg_ref, ktd_ref, el_ref):
    tt, hg_n = GDN_TT, GDN_HG
    nc = tt // CHUNK
    hg = pl.program_id(1)

    @pl.when(pl.program_id(2) == 0)
    def _():
        s_ref[...] = jnp.zeros_like(s_ref)

    lane = lax.broadcasted_iota(jnp.int32, (tt, LANES), 1)
    sub8 = lax.broadcasted_iota(jnp.int32, (8, tt), 0)
    ri = lax.broadcasted_iota(jnp.int32, (1, CHUNK, CHUNK), 1)
    ci = lax.broadcasted_iota(jnp.int32, (1, CHUNK, CHUNK), 2)
    lower_incl = ri >= ci
    strict = ri > ci
    qscale = GDN_HEAD_DIM ** -0.5
    gb = gb_ref[...]
    rows8 = gbt_ref[pl.ds(pl.multiple_of(GDN_V_HEADS + hg * hg_n, 8), 8), :]
    chunks = lambda a: a.reshape(nc, CHUNK, a.shape[-1])
    lane_chunks = lambda a: jnp.stack([a[:, c * CHUNK:(c + 1) * CHUNK] for c in range(nc)])

    for s in range(hg_n):
        hv = hg * hg_n + s
        cols = slice((s // 2) * LANES, (s // 2 + 1) * LANES)
        if s % 2 == 0:
            q3 = chunks(q_ref[:, cols].astype(F32) * qscale)
            k3 = chunks(k_ref[:, cols].astype(F32))
            kt3 = lane_chunks(kt_ref[cols, :])
            kk = _bmm(k3.astype(BF16), kt3)
            qk = _bmm(q3.astype(BF16), kt3)
        bcol = chunks(jnp.sum(jnp.where(lane == hv, gb, 0.0), axis=1, keepdims=True))
        gcol = chunks(jnp.sum(jnp.where(lane == GDN_V_HEADS + hv, gb, 0.0), axis=1, keepdims=True))
        grow = lane_chunks(jnp.sum(jnp.where(sub8 == s, rows8, 0.0), axis=0, keepdims=True))
        v3 = chunks(v_ref[:, s * LANES:(s + 1) * LANES].astype(F32))
        decay = jnp.where(lower_incl, jnp.exp(jnp.minimum(gcol - grow, 0.0)), 0.0)
        a = jnp.where(strict, bcol * kk * decay, 0.0)
        eg = jnp.exp(gcol)
        rhs = jnp.concatenate([v3 * bcol, k3 * (bcol * eg)], axis=2)
        p = a.astype(BF16)
        rhs = rhs - _bmm(p, rhs.astype(BF16))
        for _ in range(5):
            p = _bmm(p, p).astype(BF16)
            rhs = rhs + _bmm(p, rhs.astype(BF16))
        g_last = gcol[:, CHUNK - 1:CHUNK, :]
        u_ref[s] = rhs[:, :, :LANES].reshape(tt, LANES)
        w_ref[s] = rhs[:, :, LANES:].astype(BF16).reshape(tt, LANES)
        at_ref[s] = (qk * decay).astype(BF16).reshape(tt, CHUNK)
        qg_ref[s] = (q3 * eg).astype(BF16).reshape(tt, LANES)
        ktd_ref[s] = (kt3.astype(F32) * jnp.exp(g_last - grow)).astype(BF16)
        el_ref[s] = jnp.broadcast_to(jnp.exp(g_last), (nc, 1, LANES))

    nw = nw_ref[...]

    def chunk(c, _):
        rr = pl.ds(pl.multiple_of(c * CHUNK, CHUNK), CHUNK)
        st = s_ref[...]
        st_bf = st.astype(BF16)
        v_new = u_ref[:, rr, :] - _bmm(w_ref[:, rr, :], st_bf)
        v_new_bf = v_new.astype(BF16)
        o_c = _bmm(qg_ref[:, rr, :], st_bf) + _bmm(at_ref[:, rr, :], v_new_bf)
        s_ref[...] = st * el_ref[:, c] + _bmm(ktd_ref[:, c], v_new_bf)
        o_n = o_c * lax.rsqrt(jnp.mean(o_c * o_c, axis=2, keepdims=True) + RMS_EPS) * nw
        zz = z_ref[rr, :].astype(F32)
        for s in range(hg_n):
            zs = zz[:, s * LANES:(s + 1) * LANES]
            o_ref[rr, s * LANES:(s + 1) * LANES] = (o_n[s] * (zs * jax.nn.sigmoid(zs))).astype(o_ref.dtype)
        return 0

    lax.fori_loop(0, nc, chunk, 0)


def _gdn_recurrence(qkv, kt, proj, gb, gbt, norm_w_row, bsz, t):
    tt, hg_n = GDN_TT, GDN_HG
    nc = tt // CHUNK
    qw = (hg_n // 2) * LANES
    vw = hg_n * LANES
    q_blk, v_blk, z_blk = GDN_KEY_DIM // qw, 2 * GDN_KEY_DIM // vw, GDN_QKV_DIM // vw
    return pl.pallas_call(
        _gdn_kernel,
        grid=(bsz, GDN_V_HEADS // hg_n, t // tt),
        in_specs=[pl.BlockSpec((None, tt, qw), lambda b, j, i: (b, i, j)),
                  pl.BlockSpec((None, tt, qw), lambda b, j, i: (b, i, q_blk + j)),
                  pl.BlockSpec((None, qw, tt), lambda b, j, i: (b, q_blk + j, i)),
                  pl.BlockSpec((None, tt, vw), lambda b, j, i: (b, i, v_blk + j)),
                  pl.BlockSpec((None, tt, vw), lambda b, j, i: (b, i, z_blk + j)),
                  pl.BlockSpec((None, tt, LANES), lambda b, j, i: (b, i, 0)),
                  pl.BlockSpec((None, LANES, tt), lambda b, j, i: (b, 0, i)),
                  pl.BlockSpec((1, LANES), lambda b, j, i: (0, 0))],
        out_specs=pl.BlockSpec((None, tt, vw), lambda b, j, i: (b, i, j)),
        out_shape=jax.ShapeDtypeStruct((bsz, t, GDN_VAL_DIM), BF16),
        scratch_shapes=[pltpu.VMEM((hg_n, GDN_HEAD_DIM, GDN_HEAD_DIM), F32),
                        pltpu.VMEM((hg_n, tt, LANES), F32),
                        pltpu.VMEM((hg_n, tt, LANES), BF16),
                        pltpu.VMEM((hg_n, tt, CHUNK), BF16),
                        pltpu.VMEM((hg_n, tt, LANES), BF16),
                        pltpu.VMEM((hg_n, nc, GDN_HEAD_DIM, CHUNK), BF16),
                        pltpu.VMEM((hg_n, nc, 1, LANES), F32)],
        compiler_params=_cparams("parallel", "parallel", "arbitrary"),
        name="gdn_recurrence",
    )(qkv, qkv, kt, qkv, proj, gb, gbt, norm_w_row)


def _layer_norm(r, g, b):
    mu = jnp.mean(r, axis=1, keepdims=True)
    d = r - mu
    var = jnp.mean(d * d, axis=1, keepdims=True)
    return d * lax.rsqrt(var + LN_EPS) * g + b


def _post_mix_kernel(o_ref, w_ref, x_ref, g_ref, b_ref, rw_ref, rb_ref, hx_ref, cnt_ref, *, tm):
    @pl.when(pl.program_id(0) == 0)
    def _():
        cnt_ref[...] = jnp.zeros_like(cnt_ref)

    mix = _dot(o_ref[...], w_ref[...])
    h = _layer_norm(DEEPNORM_ALPHA * x_ref[...] + mix, g_ref[...], b_ref[...])
    hx_ref[:, :D_MODEL] = h

    logits = jnp.dot(h, rw_ref[...], precision=lax.Precision.HIGHEST, preferred_element_type=F32)
    scores = jax.nn.sigmoid(logits)
    lane = lax.broadcasted_iota(jnp.int32, (tm, LANES), 1)
    neg = -jnp.inf
    sel = jnp.where(lane < N_EXPERTS, scores + rb_ref[...], neg)
    grp = lane // EXPERTS_PER_GROUP

    def top2(x):
        m1 = jnp.max(x, axis=1, keepdims=True)
        i1 = jnp.min(jnp.where(x == m1, lane, LANES), axis=1, keepdims=True)
        x2 = jnp.where(lane == i1, neg, x)
        m2 = jnp.max(x2, axis=1, keepdims=True)
        i2 = jnp.min(jnp.where(x2 == m2, lane, LANES), axis=1, keepdims=True)
        return m1, i1, m2, i2

    best = jnp.zeros((tm, 1), jnp.int32)
    best_score = None
    for gidx in range(N_GROUPS):
        m1, _, m2, _ = top2(jnp.where(grp == gidx, sel, neg))
        gs = m1 + m2
        if best_score is None:
            best_score = gs
        else:
            upd = gs > best_score
            best = jnp.where(upd, gidx, best)
            best_score = jnp.where(upd, gs, best_score)
    _, i1, _, i2 = top2(jnp.where(grp == best, sel, neg))
    s1 = jnp.sum(jnp.where(lane == i1, scores, 0.0), axis=1, keepdims=True)
    s2 = jnp.sum(jnp.where(lane == i2, scores, 0.0), axis=1, keepdims=True)
    denom = s1 + s2
    w1, w2 = s1 / denom, s2 / denom
    first_lower = i1 < i2
    w_lo = jnp.where(first_lower, w1, w2)
    w_hi = jnp.where(first_lower, w2, w1)
    a = jnp.minimum(i1, i2) % EXPERTS_PER_GROUP
    bb = jnp.maximum(i1, i2) % EXPERTS_PER_GROUP
    pair = jnp.where(a == 0, bb - 1, jnp.where(a == 1, bb + 1, 5))
    bucket = best * 6 + pair

    onehot = jnp.where(lane == bucket, 1.0, 0.0)
    r = lax.broadcasted_iota(jnp.int32, (tm, tm), 0)
    c = lax.broadcasted_iota(jnp.int32, (tm, tm), 1)
    before = jnp.where(c < r, 1.0, 0.0).astype(BF16)
    prefix = _dot(before, onehot.astype(BF16)) + cnt_ref[...]
    rank = jnp.sum(jnp.where(lane == bucket, prefix, 0.0), axis=1, keepdims=True)
    cnt_ref[...] += jnp.sum(onehot, axis=0, keepdims=True)

    tail = jnp.where(lane == 0, w_lo, jnp.where(lane == 1, w_hi, jnp.where(
        lane == 2, bucket.astype(F32), jnp.where(lane == 3, rank, 0.0))))
    hx_ref[:, D_MODEL:] = tail


def _post_mix(o, w_out, x, ln_g, ln_b, router_w_pad, router_b_pad):
    n, ko = o.shape
    tm = 512
    kern = functools.partial(_post_mix_kernel, tm=tm)
    row = lambda i: (0, 0)
    return pl.pallas_call(
        kern,
        grid=(n // tm,),
        in_specs=[pl.BlockSpec((tm, ko), lambda i: (i, 0)),
                  pl.BlockSpec((ko, D_MODEL), row),
                  pl.BlockSpec((tm, D_MODEL), lambda i: (i, 0)),
                  pl.BlockSpec((1, D_MODEL), row),
                  pl.BlockSpec((1, D_MODEL), row),
                  pl.BlockSpec((D_MODEL, LANES), row),
                  pl.BlockSpec((1, LANES), row)],
        out_specs=[pl.BlockSpec((tm, D_MODEL + TAIL), lambda i: (i, 0)),
                   pl.BlockSpec((1, LANES), row)],
        out_shape=[jax.ShapeDtypeStruct((n, D_MODEL + TAIL), F32),
                   jax.ShapeDtypeStruct((1, LANES), F32)],
        compiler_params=_cparams("arbitrary"),
        name="post_mix",
    )(o, w_out, x, ln_g, ln_b, router_w_pad, router_b_pad)


PERM_CHUNK = 512
PERM_UNROLL = 8


def _row_copy(src_ref, dst_ref, sem, s, d):
    return pltpu.make_async_copy(src_ref.at[pl.ds(s, 1)], dst_ref.at[pl.ds(d, 1)], sem)


def _permute_rows_loops(copy_row):
    def issue(i, _):
        for u in range(PERM_UNROLL):
            copy_row(i * PERM_UNROLL + u).start()
        return 0

    def drain(i, _):
        for u in range(PERM_UNROLL):
            copy_row(0).wait()
        return 0

    lax.fori_loop(0, PERM_CHUNK // PERM_UNROLL, issue, 0)
    lax.fori_loop(0, PERM_CHUNK // PERM_UNROLL, drain, 0)


def _scatter_rows_kernel(idx_ref, src_ref, init_ref, out_ref, sem):
    del init_ref
    _permute_rows_loops(lambda r: _row_copy(src_ref, out_ref, sem, r, idx_ref[r]))


def _scatter_rows(src, dest, n_out):
    n, w = src.shape
    return pl.pallas_call(
        _scatter_rows_kernel,
        grid=(n // PERM_CHUNK,),
        in_specs=[pl.BlockSpec((PERM_CHUNK,), lambda i: (i,), memory_space=pltpu.SMEM),
                  pl.BlockSpec((PERM_CHUNK, w), lambda i: (i, 0)),
                  pl.BlockSpec(memory_space=pl.ANY)],
        out_specs=pl.BlockSpec(memory_space=pl.ANY),
        out_shape=jax.ShapeDtypeStruct((n_out, w), src.dtype),
        scratch_shapes=[pltpu.SemaphoreType.DMA(())],
        input_output_aliases={2: 0},
        compiler_params=pltpu.CompilerParams(dimension_semantics=("arbitrary",), has_side_effects=True),
        name="scatter_rows",
    )(dest, src, jnp.zeros((n_out, w), src.dtype))


def _gather_rows_kernel(idx_ref, src_ref, out_ref, sem):
    _permute_rows_loops(lambda r: _row_copy(src_ref, out_ref, sem, idx_ref[r], r))


def _gather_rows(src, idx):
    n = idx.shape[0]
    w = src.shape[1]
    return pl.pallas_call(
        _gather_rows_kernel,
        grid=(n // PERM_CHUNK,),
        in_specs=[pl.BlockSpec((PERM_CHUNK,), lambda i: (i,), memory_space=pltpu.SMEM),
                  pl.BlockSpec(memory_space=pl.ANY)],
        out_specs=pl.BlockSpec((PERM_CHUNK, w), lambda i: (i, 0)),
        out_shape=jax.ShapeDtypeStruct((n, w), src.dtype),
        scratch_shapes=[pltpu.SemaphoreType.DMA(())],
        compiler_params=pltpu.CompilerParams(dimension_semantics=("arbitrary",), has_side_effects=True),
        name="gather_rows",
    )(idx, src)


MOE_TILE = 256


def _moe_kernel(src_ref, ea_ref, eb_ref, nused_ref, xs_ref, gua_ref, gub_ref, da_ref, db_ref, y_ref):
    del src_ref, ea_ref, eb_ref

    @pl.when(pl.program_id(0) >= nused_ref[0])
    def _():
        y_ref[...] = jnp.zeros_like(y_ref)

    @pl.when(pl.program_id(0) < nused_ref[0])
    def _():
        x = xs_ref[:, :D_MODEL].astype(BF16)
        tail = xs_ref[:, D_MODEL:]
        lane = lax.broadcasted_iota(jnp.int32, tail.shape, 1)
        w_lo = jnp.sum(jnp.where(lane == 0, tail, 0.0), axis=1, keepdims=True)
        w_hi = jnp.sum(jnp.where(lane == 1, tail, 0.0), axis=1, keepdims=True)

        def expert(gu_ref, d_ref):
            gu = _dot(x, gu_ref[...])
            g_in, u_in = gu[:, :D_EXPERT], gu[:, D_EXPERT:]
            act = (g_in * jax.nn.sigmoid(g_in)) * u_in
            return _dot(act.astype(BF16), d_ref[...])

        y_ref[...] = w_lo * expert(gua_ref, da_ref) + w_hi * expert(gub_ref, db_ref)


def _moe_ffn(xs, tile_src, tile_ea, tile_eb, n_used, w_gate_up, w_down):
    n_pad = xs.shape[0]
    n_tiles = n_pad // MOE_TILE
    grid_spec = pltpu.PrefetchScalarGridSpec(
        num_scalar_prefetch=4,
        grid=(n_tiles,),
        in_specs=[pl.BlockSpec((MOE_TILE, D_MODEL + TAIL), lambda t, src, ea, eb, nu: (src[t], 0)),
                  pl.BlockSpec((None, D_MODEL, 2 * D_EXPERT), lambda t, src, ea, eb, nu: (ea[t], 0, 0)),
                  pl.BlockSpec((None, D_MODEL, 2 * D_EXPERT), lambda t, src, ea, eb, nu: (eb[t], 0, 0)),
                  pl.BlockSpec((None, D_EXPERT, D_MODEL), lambda t, src, ea, eb, nu: (ea[t], 0, 0)),
                  pl.BlockSpec((None, D_EXPERT, D_MODEL), lambda t, src, ea, eb, nu: (eb[t], 0, 0))],
        out_specs=pl.BlockSpec((MOE_TILE, D_MODEL), lambda t, src, ea, eb, nu: (t, 0)),
    )
    return pl.pallas_call(
        _moe_kernel,
        grid_spec=grid_spec,
        out_shape=jax.ShapeDtypeStruct((n_pad, D_MODEL), F32),
        compiler_params=_cparams("arbitrary"),
        name="moe_ffn",
    )(tile_src, tile_ea, tile_eb, n_used, xs, w_gate_up, w_gate_up, w_down, w_down)


def _routing_plan(hx, counts, n):
    n_tiles = n // MOE_TILE + N_BUCKETS
    cnt = counts[0, :N_BUCKETS].astype(jnp.int32)
    tiles_per = (cnt + MOE_TILE - 1) // MOE_TILE
    tile_end = jnp.cumsum(tiles_per)
    tile_start = tile_end - tiles_per
    bucket = hx[:, D_MODEL + 2].astype(jnp.int32)
    rank = hx[:, D_MODEL + 3].astype(jnp.int32)
    dest = tile_start[bucket] * MOE_TILE + rank
    n_used = tile_end[-1]
    tile_id = jnp.minimum(jnp.arange(n_tiles, dtype=jnp.int32), n_used - 1)
    tile_bucket = jnp.sum(tile_id[:, None] >= tile_end[None, :], axis=1).astype(jnp.int32)
    group = tile_bucket // 6
    pair = tile_bucket % 6
    lo = jnp.where(pair < 3, 0, jnp.where(pair < 5, 1, 2))
    hi = jnp.where(pair < 3, pair + 1, jnp.where(pair < 5, pair - 1, 3))
    return dest, tile_id, group * 4 + lo, group * 4 + hi, n_used.reshape(1).astype(jnp.int32), n_tiles * MOE_TILE


def _final_kernel(hx_ref, moe_ref, p_ref, wg_ref, wp_ref, g_ref, b_ref, o_ref):
    h = hx_ref[...]
    gate = jax.nn.sigmoid(_dot(h.astype(BF16), wg_ref[...]))
    ple = gate * _dot(p_ref[...].astype(BF16), wp_ref[...])
    o_ref[...] = _layer_norm(DEEPNORM_ALPHA * h + moe_ref[...] + ple, g_ref[...], b_ref[...])


def _final(hx, moe, p, w_gate, w_proj, ln_g, ln_b):
    n = moe.shape[0]
    tm = 512
    row = lambda i: (0, 0)
    return pl.pallas_call(
        _final_kernel,
        grid=(n // tm,),
        in_specs=[pl.BlockSpec((tm, D_MODEL), lambda i: (i, 0)),
                  pl.BlockSpec((tm, D_MODEL), lambda i: (i, 0)),
                  pl.BlockSpec((tm, PLE_DIM), lambda i: (i, 0)),
                  pl.BlockSpec((D_MODEL, D_MODEL), row),
                  pl.BlockSpec((PLE_DIM, D_MODEL), row),
                  pl.BlockSpec((1, D_MODEL), row),
                  pl.BlockSpec((1, D_MODEL), row)],
        out_specs=pl.BlockSpec((tm, D_MODEL), lambda i: (i, 0)),
        out_shape=jax.ShapeDtypeStruct((n, D_MODEL), F32),
        compiler_params=_cparams("parallel"),
        name="final",
    )(hx, moe, p, w_gate, w_proj, ln_g, ln_b)


def _pad_lanes(a, width=LANES):
    return jnp.pad(a, [(0, 0)] * (a.ndim - 1) + [(0, width - a.shape[-1])])


def _moe_and_norm(o, w_out, x, p_i, ln1_g, ln1_b, rw_pad, rb_pad, w_gate_up, w_down, ple_w_gate, ple_w_proj,
                  ln2_g, ln2_b):
    n = x.shape[0]
    hx, counts = _post_mix(o, w_out.astype(BF16), x, ln1_g[None], ln1_b[None], rw_pad, rb_pad)
    dest, tile_src, tile_ea, tile_eb, n_used, n_pad = _routing_plan(hx, counts, n)
    xs = _scatter_rows(hx, dest, n_pad)
    ys = _moe_ffn(xs, tile_src, tile_ea, tile_eb, n_used, w_gate_up.astype(BF16), w_down.astype(BF16))
    moe = _gather_rows(ys, dest)
    return _final(hx, moe, p_i, ple_w_gate.astype(BF16), ple_w_proj.astype(BF16), ln2_g[None], ln2_b[None])


def kernel(x, p, sb_w_in, sb_w_out, gdn_w_in, gdn_conv_w, gdn_a_log, gdn_dt_bias, gdn_norm_w, gdn_w_out,
           ln1_g, ln1_b, router_w, router_b, expert_w_gate_up, expert_w_down, ple_w_gate, ple_w_proj,
           ln2_g, ln2_b):
    bsz, t, d = x.shape
    n = bsz * t
    x = x.reshape(n, d)
    p = p.reshape(DEPTH, n, PLE_DIM)
    rw_pad = _pad_lanes(router_w)
    rb_pad = _pad_lanes(router_b[None])

    qkv = _linear(x, sb_w_in[0].astype(BF16), BF16, 1024, 1024)
    o = _sb_attention(qkv.reshape(bsz, t, 3 * d), bsz, t).reshape(n, d)
    x = _moe_and_norm(o, sb_w_out[0], x, p[0], ln1_g[0], ln1_b[0], rw_pad, rb_pad, expert_w_gate_up[0],
                      expert_w_down[0], ple_w_gate[0], ple_w_proj[0], ln2_g[0], ln2_b[0])

    w_in = gdn_w_in[0]
    split = GDN_QKV_DIM + GDN_VAL_DIM
    proj = _linear(x, w_in[:, :split].astype(BF16), BF16, 1024, 1024)
    ba = _linear(x, _pad_lanes(w_in[:, split:]).astype(BF16), F32, 1024, LANES)
    proj = proj.reshape(bsz, t, split)
    qkv_c, kt = _gdn_conv(proj, gdn_conv_w[0], bsz, t)
    head_pad = lambda a: jnp.pad(a, (GDN_V_HEADS, LANES - 2 * GDN_V_HEADS))[None]
    gb, gbt = _gdn_gate_prep(ba.reshape(bsz, t, LANES), head_pad(gdn_a_log[0]), head_pad(gdn_dt_bias[0]), bsz, t)
    o = _gdn_recurrence(qkv_c, kt, proj, gb, gbt, gdn_norm_w[0][None], bsz, t).reshape(n, GDN_VAL_DIM)
    x = _moe_and_norm(o, gdn_w_out[0], x, p[1], ln1_g[1], ln1_b[1], rw_pad, rb_pad, expert_w_gate_up[1],
                      expert_w_down[1], ple_w_gate[1], ple_w_proj[1], ln2_g[1], ln2_b[1])
    return x.reshape(bsz, t, d)
```

```python
import functools

import jax
import jax.numpy as jnp
from jax import lax
from jax.experimental import pallas as pl
from jax.experimental.pallas import tpu as pltpu

D_MODEL = 1024
DEPTH = 2
PLE_DIM = 256
SB_HEADS = 16
SB_HEAD_DIM = 64
GDN_QK_HEADS = 8
GDN_V_HEADS = 16
GDN_HEAD_DIM = 128
GDN_KEY_DIM = GDN_QK_HEADS * GDN_HEAD_DIM
GDN_VAL_DIM = GDN_V_HEADS * GDN_HEAD_DIM
GDN_QKV_DIM = 2 * GDN_KEY_DIM + GDN_VAL_DIM
CONV_K = 4
CHUNK = 64
N_EXPERTS = 16
N_GROUPS = 4
EXPERTS_PER_GROUP = 4
D_EXPERT = 512
N_BUCKETS = 24
LN_EPS = 1e-5
RMS_EPS = 1e-6
DEEPNORM_ALPHA = (2 * DEPTH) ** 0.25

LANES = 128
TAIL = LANES
VMEM_LIMIT = 48 * 1024 * 1024
SB_DEAD_LOG = -104.0

F32 = jnp.float32
BF16 = jnp.bfloat16


def _cparams(*sem):
    return pltpu.CompilerParams(dimension_semantics=sem, vmem_limit_bytes=VMEM_LIMIT)


def _dot(a, b):
    return jnp.dot(a, b, preferred_element_type=F32)


def _dot_nt(a, b):
    return lax.dot_general(a, b, (((1,), (1,)), ((), ())), preferred_element_type=F32)


def _split_dot(a, b_bf16, parts):
    acc = None
    rem = a
    for _ in range(parts):
        hi = rem.astype(BF16)
        term = _dot(hi, b_bf16)
        acc = term if acc is None else acc + term
        rem = rem - hi.astype(F32)
    return acc


def _linear_kernel(x_ref, w_ref, o_ref):
    o_ref[...] = _dot(x_ref[...].astype(BF16), w_ref[...]).astype(o_ref.dtype)


def _linear(x, w, out_dtype, tm, tn):
    m, k = x.shape
    n = w.shape[1]
    return pl.pallas_call(
        _linear_kernel,
        grid=(m // tm, n // tn),
        in_specs=[pl.BlockSpec((tm, k), lambda i, j: (i, 0)),
                  pl.BlockSpec((k, tn), lambda i, j: (0, j))],
        out_specs=pl.BlockSpec((tm, tn), lambda i, j: (i, j)),
        out_shape=jax.ShapeDtypeStruct((m, n), out_dtype),
        compiler_params=_cparams("parallel", "parallel"),
        name="linear",
    )(x, w)


SB_TQ = 128
SB_TK = 128
SB_GROUP = 4
SB_PIECE = 64


def _sb_kernel(q_ref, k_ref, v_ref, o_ref, carry_ref, acc_ref, *, scale):
    t = q_ref.shape[0]
    tq, tk, grp = SB_TQ, SB_TK, SB_GROUP
    rows = 2 * tq
    nrow = grp * rows
    npiece = nrow // SB_PIECE
    lane = lax.broadcasted_iota(jnp.int32, (tq, LANES), 1)
    first_head = lane < SB_HEAD_DIM
    jj = lax.broadcasted_iota(jnp.int32, (tk, 2 * tk), 0)
    ss = lax.broadcasted_iota(jnp.int32, (tk, 2 * tk), 1)
    cum_mat = jnp.where((jj > ss) | (ss >= tk), 1.0, 0.0).astype(BF16)
    r = lax.broadcasted_iota(jnp.int32, (nrow, tk), 0)
    row_in_tile = r % tq
    tile_of_row = r // rows
    col = lax.broadcasted_iota(jnp.int32, (nrow, tk), 1)
    piece = lambda a, p: a[p * SB_PIECE:(p + 1) * SB_PIECE]

    def group(gi, _):
        first_tile = gi * grp
        qs = []
        for a in range(grp):
            q = q_ref[pl.ds(pl.multiple_of((first_tile + a) * tq, tq), tq), :] * scale
            zero = jnp.zeros_like(q)
            qs += [jnp.where(first_head, q, zero), jnp.where(first_head, zero, q)]
        qq = jnp.concatenate(qs, axis=0).reshape(grp, rows, LANES)
        carry_ref[...] = jnp.zeros_like(carry_ref)
        acc_ref[...] = jnp.zeros_like(acc_ref)

        def alive(c):
            d, top = c
            return jnp.logical_and(d < first_tile + grp, top >= SB_DEAD_LOG)

        def step(d, diagonal):
            ks, vs = [], []
            for a in range(grp):
                k0 = pl.multiple_of(jnp.maximum(first_tile + a - d, 0) * tk, tk)
                ks.append(k_ref[pl.ds(k0, tk), :])
                vs.append(v_ref[pl.ds(k0, tk), :])
            z = jnp.einsum("gqd,gkd->gqk", qq, jnp.stack(ks), preferred_element_type=F32).reshape(nrow, tk)
            if diagonal:
                causal = col < row_in_tile
            else:
                causal = tile_of_row >= d - first_tile
            his, los, lps = [], [], []
            for p in range(npiece):
                zp = piece(z, p)
                mp = piece(causal, p)
                sp = jnp.maximum(zp, 0.0) + jnp.log(1.0 + jnp.exp(-jnp.abs(zp)))
                lm = jnp.where(mp, -sp, 0.0)
                hi = lm.astype(BF16)
                his.append(hi)
                los.append((lm - hi.astype(F32)).astype(BF16))
                lps.append(jnp.where(mp, zp - sp, -1e30))
            cs = _dot(jnp.concatenate(his, axis=0), cum_mat) + _dot(jnp.concatenate(los, axis=0), cum_mat)
            attn, top = [], None
            for p in range(npiece):
                rr = pl.ds(p * SB_PIECE, SB_PIECE)
                csp = piece(cs, p)
                carry = carry_ref[rr, :]
                attn.append(jnp.exp(lps[p] + csp[:, :tk] + carry).astype(BF16))
                carry = carry + csp[:, tk:]
                carry_ref[rr, :] = carry
                top = carry if top is None else jnp.maximum(top, carry)
            av = jnp.einsum("gqk,gkd->gqd", jnp.concatenate(attn, axis=0).reshape(grp, rows, tk), jnp.stack(vs),
                            preferred_element_type=F32)
            acc_ref[...] += av.reshape(nrow, LANES)
            return d + 1, jnp.max(top)

        lax.while_loop(alive, lambda c: step(c[0], False), step(jnp.int32(0), True))
        for a in range(grp):
            acc = acc_ref[a * rows:(a + 1) * rows, :]
            o_ref[pl.ds(pl.multiple_of((first_tile + a) * tq, tq), tq), :] = jnp.where(
                first_head, acc[:tq], acc[tq:]).astype(o_ref.dtype)
        return 0

    lax.fori_loop(0, t // (tq * grp), group, 0)


def _sb_attention(qkv, bsz, t):
    ncol = D_MODEL // LANES
    nrow = SB_GROUP * 2 * SB_TQ
    kern = functools.partial(_sb_kernel, scale=SB_HEAD_DIM ** -0.5)
    return pl.pallas_call(
        kern,
        grid=(bsz, ncol),
        in_specs=[pl.BlockSpec((None, t, LANES), lambda b, h: (b, 0, h)),
                  pl.BlockSpec((None, t, LANES), lambda b, h: (b, 0, ncol + h)),
                  pl.BlockSpec((None, t, LANES), lambda b, h: (b, 0, 2 * ncol + h))],
        out_specs=pl.BlockSpec((None, t, LANES), lambda b, h: (b, 0, h)),
        out_shape=jax.ShapeDtypeStruct((bsz, t, D_MODEL), BF16),
        scratch_shapes=[pltpu.VMEM((nrow, SB_TK), F32), pltpu.VMEM((nrow, LANES), F32)],
        compiler_params=_cparams("parallel", "parallel"),
        name="sb_attention",
    )(qkv, qkv, qkv)


CONV_PIECE = 128


def _conv_kernel(x_ref, halo_ref, w_ref, o_ref, kt_ref, *, tt, tc):
    i = pl.program_id(1)
    j = pl.program_id(2)
    w = w_ref[...]
    q_blocks = GDN_KEY_DIM // tc

    def run(normalise, transpose):
        for pc in range(tt // CONV_PIECE):
            rows = slice(pc * CONV_PIECE, (pc + 1) * CONV_PIECE)
            if pc == 0:
                prev = jnp.where(i > 0, halo_ref[...].astype(F32), 0.0)
            else:
                prev = x_ref[pc * CONV_PIECE - 8:pc * CONV_PIECE, :].astype(F32)
            cur = x_ref[rows, :].astype(F32)
            ext = jnp.concatenate([prev, cur], axis=0)
            acc = cur * w[CONV_K - 1:CONV_K, :]
            for s in range(1, CONV_K):
                acc = acc + pltpu.roll(ext, s, 0)[8:] * w[CONV_K - 1 - s:CONV_K - s, :]
            y = acc * jax.nn.sigmoid(acc)
            if normalise:
                parts = []
                for h in range(tc // GDN_HEAD_DIM):
                    yh = y[:, h * GDN_HEAD_DIM:(h + 1) * GDN_HEAD_DIM]
                    parts.append(yh * lax.rsqrt(jnp.sum(yh * yh, axis=1, keepdims=True) + RMS_EPS))
                y = jnp.concatenate(parts, axis=1)
            o_ref[rows, :] = y.astype(o_ref.dtype)
            if transpose:
                kt_ref[:, rows] = y.T.astype(kt_ref.dtype)

    @pl.when(j < q_blocks)
    def _():
        run(True, False)

    @pl.when((j >= q_blocks) & (j < 2 * q_blocks))
    def _():
        run(True, True)

    @pl.when(j >= 2 * q_blocks)
    def _():
        run(False, False)


def _gdn_conv(proj, conv_w, bsz, t):
    tt, tc = 2048, 256
    kern = functools.partial(_conv_kernel, tt=tt, tc=tc)
    q_blocks = GDN_KEY_DIM // tc
    return pl.pallas_call(
        kern,
        grid=(bsz, t // tt, GDN_QKV_DIM // tc),
        in_specs=[pl.BlockSpec((None, tt, tc), lambda b, i, j: (b, i, j)),
                  pl.BlockSpec((None, 8, tc), lambda b, i, j: (b, jnp.maximum(i * (tt // 8) - 1, 0), j)),
                  pl.BlockSpec((CONV_K, tc), lambda b, i, j: (0, j))],
        out_specs=[pl.BlockSpec((None, tt, tc), lambda b, i, j: (b, i, j)),
                   pl.BlockSpec((None, tc, tt), lambda b, i, j: (b, jnp.clip(j - q_blocks, 0, q_blocks - 1), i))],
        out_shape=[jax.ShapeDtypeStruct((bsz, t, GDN_QKV_DIM), BF16),
                   jax.ShapeDtypeStruct((bsz, GDN_KEY_DIM, t), BF16)],
        compiler_params=_cparams("parallel", "parallel", "arbitrary"),
        name="gdn_conv",
    )(proj, proj, conv_w)


def _gate_prep_kernel(ba_ref, alog_ref, dtb_ref, gb_ref, gbt_ref, *, tt):
    ba = ba_ref[...]
    lane = lax.broadcasted_iota(jnp.int32, (tt, LANES), 1)
    beta = jax.nn.sigmoid(ba)
    xa = ba + dtb_ref[...]
    softplus = jnp.maximum(xa, 0.0) + jnp.log1p(jnp.exp(-jnp.abs(xa)))
    g = -jnp.exp(alog_ref[...]) * softplus
    r = lax.broadcasted_iota(jnp.int32, (tt, tt), 0)
    c = lax.broadcasted_iota(jnp.int32, (tt, tt), 1)
    tri = jnp.where((c <= r) & (c // CHUNK == r // CHUNK), 1.0, 0.0).astype(BF16)
    is_g = (lane >= GDN_V_HEADS) & (lane < 2 * GDN_V_HEADS)
    g = jnp.where(is_g, g, 0.0)
    rem = g
    gc = jnp.zeros_like(g)
    for _ in range(3):
        hi = rem.astype(BF16)
        gc = gc + _dot(tri, hi)
        rem = rem - hi.astype(F32)
    out = jnp.where(lane < GDN_V_HEADS, beta, gc)
    gb_ref[...] = out
    gbt_ref[...] = out.T


def _gdn_gate_prep(ba, a_log_row, dt_bias_row, bsz, t):
    tt = 512
    kern = functools.partial(_gate_prep_kernel, tt=tt)
    return pl.pallas_call(
        kern,
        grid=(bsz, t // tt),
        in_specs=[pl.BlockSpec((None, tt, LANES), lambda b, i: (b, i, 0)),
                  pl.BlockSpec((1, LANES), lambda b, i: (0, 0)),
                  pl.BlockSpec((1, LANES), lambda b, i: (0, 0))],
        out_specs=[pl.BlockSpec((None, tt, LANES), lambda b, i: (b, i, 0)),
                   pl.BlockSpec((None, LANES, tt), lambda b, i: (b, 0, i))],
        out_shape=[jax.ShapeDtypeStruct((bsz, t, LANES), F32),
                   jax.ShapeDtypeStruct((bsz, LANES, t), F32)],
        compiler_params=_cparams("parallel", "parallel"),
        name="gdn_gate_prep",
    )(ba, a_log_row, dt_bias_row)


GDN_TT = 512
GDN_HG = 8


def _bmm(a, b):
    return jnp.einsum("cik,ckj->cij", a, b, preferred_element_type=F32)


def _gdn_kernel(q_ref, k_ref, kt_ref, v_ref, z_ref, gb_ref, gbt_ref, nw_ref, o_ref,
                s_ref, u_ref, w_ref, at_ref, qg_ref, ktd_ref, el_ref):
    tt, hg_n = GDN_TT, GDN_HG
    nc = tt // CHUNK
    hg = pl.program_id(1)

    @pl.when(pl.program_id(2) == 0)
    def _():
        s_ref[...] = jnp.zeros_like(s_ref)

    lane = lax.broadcasted_iota(jnp.int32, (tt, LANES), 1)
    sub8 = lax.broadcasted_iota(jnp.int32, (8, tt), 0)
    ri = lax.broadcasted_iota(jnp.int32, (1, CHUNK, CHUNK), 1)
    ci = lax.broadcasted_iota(jnp.int32, (1, CHUNK, CHUNK), 2)
    lower_incl = ri >= ci
    strict = ri > ci
    qscale = GDN_HEAD_DIM ** -0.5
    gb = gb_ref[...]
    rows8 = gbt_ref[pl.ds(pl.multiple_of(GDN_V_HEADS + hg * hg_n, 8), 8), :]
    chunks = lambda a: a.reshape(nc, CHUNK, a.shape[-1])
    lane_chunks = lambda a: jnp.stack([a[:, c * CHUNK:(c + 1) * CHUNK] for c in range(nc)])

    for s in range(hg_n):
        hv = hg * hg_n + s
        cols = slice((s // 2) * LANES, (s // 2 + 1) * LANES)
        if s % 2 == 0:
            q3 = chunks(q_ref[:, cols].astype(F32) * qscale)
            k3 = chunks(k_ref[:, cols].astype(F32))
            kt3 = lane_chunks(kt_ref[cols, :])
            kk = _bmm(k3.astype(BF16), kt3)
            qk = _bmm(q3.astype(BF16), kt3)
        bcol = chunks(jnp.sum(jnp.where(lane == hv, gb, 0.0), axis=1, keepdims=True))
        gcol = chunks(jnp.sum(jnp.where(lane == GDN_V_HEADS + hv, gb, 0.0), axis=1, keepdims=True))
        grow = lane_chunks(jnp.sum(jnp.where(sub8 == s, rows8, 0.0), axis=0, keepdims=True))
        v3 = chunks(v_ref[:, s * LANES:(s + 1) * LANES].astype(F32))
        decay = jnp.where(lower_incl, jnp.exp(jnp.minimum(gcol - grow, 0.0)), 0.0)
        a = jnp.where(strict, bcol * kk * decay, 0.0)
        eg = jnp.exp(gcol)
        rhs = jnp.concatenate([v3 * bcol, k3 * (bcol * eg)], axis=2)
        p = a.astype(BF16)
        rhs = rhs - _bmm(p, rhs.astype(BF16))
        for _ in range(5):
            p = _bmm(p, p).astype(BF16)
            rhs = rhs + _bmm(p, rhs.astype(BF16))
        g_last = gcol[:, CHUNK - 1:CHUNK, :]
        u_ref[s] = rhs[:, :, :LANES].reshape(tt, LANES)
        w_ref[s] = rhs[:, :, LANES:].astype(BF16).reshape(tt, LANES)
        at_ref[s] = (qk * decay).astype(BF16).reshape(tt, CHUNK)
        qg_ref[s] = (q3 * eg).astype(BF16).reshape(tt, LANES)
        ktd_ref[s] = (kt3.astype(F32) * jnp.exp(g_last - grow)).astype(BF16)
        el_ref[s] = jnp.broadcast_to(jnp.exp(g_last), (nc, 1, LANES))

    nw = nw_ref[...]

    def chunk(c, _):
        rr = pl.ds(pl.multiple_of(c * CHUNK, CHUNK), CHUNK)
        st = s_ref[...]
        st_bf = st.astype(BF16)
        v_new = u_ref[:, rr, :] - _bmm(w_ref[:, rr, :], st_bf)
        v_new_bf = v_new.astype(BF16)
        o_c = _bmm(qg_ref[:, rr, :], st_bf) + _bmm(at_ref[:, rr, :], v_new_bf)
        s_ref[...] = st * el_ref[:, c] + _bmm(ktd_ref[:, c], v_new_bf)
        o_n = o_c * lax.rsqrt(jnp.mean(o_c * o_c, axis=2, keepdims=True) + RMS_EPS) * nw
        zz = z_ref[rr, :].astype(F32)
        for s in range(hg_n):
            zs = zz[:, s * LANES:(s + 1) * LANES]
            o_ref[rr, s * LANES:(s + 1) * LANES] = (o_n[s] * (zs * jax.nn.sigmoid(zs))).astype(o_ref.dtype)
        return 0

    lax.fori_loop(0, nc, chunk, 0)


def _gdn_recurrence(qkv, kt, proj, gb, gbt, norm_w_row, bsz, t):
    tt, hg_n = GDN_TT, GDN_HG
    nc = tt // CHUNK
    qw = (hg_n // 2) * LANES
    vw = hg_n * LANES
    q_blk, v_blk, z_blk = GDN_KEY_DIM // qw, 2 * GDN_KEY_DIM // vw, GDN_QKV_DIM // vw
    return pl.pallas_call(
        _gdn_kernel,
        grid=(bsz, GDN_V_HEADS // hg_n, t // tt),
        in_specs=[pl.BlockSpec((None, tt, qw), lambda b, j, i: (b, i, j)),
                  pl.BlockSpec((None, tt, qw), lambda b, j, i: (b, i, q_blk + j)),
                  pl.BlockSpec((None, qw, tt), lambda b, j, i: (b, j, i)),
                  pl.BlockSpec((None, tt, vw), lambda b, j, i: (b, i, v_blk + j)),
                  pl.BlockSpec((None, tt, vw), lambda b, j, i: (b, i, z_blk + j)),
                  pl.BlockSpec((None, tt, LANES), lambda b, j, i: (b, i, 0)),
                  pl.BlockSpec((None, LANES, tt), lambda b, j, i: (b, 0, i)),
                  pl.BlockSpec((1, LANES), lambda b, j, i: (0, 0))],
        out_specs=pl.BlockSpec((None, tt, vw), lambda b, j, i: (b, i, j)),
        out_shape=jax.ShapeDtypeStruct((bsz, t, GDN_VAL_DIM), BF16),
        scratch_shapes=[pltpu.VMEM((hg_n, GDN_HEAD_DIM, GDN_HEAD_DIM), F32),
                        pltpu.VMEM((hg_n, tt, LANES), F32),
                        pltpu.VMEM((hg_n, tt, LANES), BF16),
                        pltpu.VMEM((hg_n, tt, CHUNK), BF16),
                        pltpu.VMEM((hg_n, tt, LANES), BF16),
                        pltpu.VMEM((hg_n, nc, GDN_HEAD_DIM, CHUNK), BF16),
                        pltpu.VMEM((hg_n, nc, 1, LANES), F32)],
        compiler_params=_cparams("parallel", "parallel", "arbitrary"),
        name="gdn_recurrence",
    )(qkv, qkv, kt, qkv, proj, gb, gbt, norm_w_row)


def _layer_norm(r, g, b):
    mu = jnp.mean(r, axis=1, keepdims=True)
    d = r - mu
    var = jnp.mean(d * d, axis=1, keepdims=True)
    return d * lax.rsqrt(var + LN_EPS) * g + b


CNT_ROWS = 32


def _post_mix_kernel(o_ref, w_ref, x_ref, g_ref, b_ref, rwh_ref, rwl_ref, rb_ref, hx_ref, rec_ref, cnt_ref, *, tm):
    @pl.when(pl.program_id(0) == 0)
    def _():
        cnt_ref[...] = jnp.zeros_like(cnt_ref)

    mix = _dot(o_ref[...], w_ref[...])
    h = _layer_norm(DEEPNORM_ALPHA * x_ref[...] + mix, g_ref[...], b_ref[...])
    hx_ref[:, :D_MODEL] = h

    h_hi = h.astype(BF16)
    h_lo = (h - h_hi.astype(F32)).astype(BF16)
    logits = _dot(h_hi, rwh_ref[...]) + _dot(h_lo, rwh_ref[...]) + _dot(h_hi, rwl_ref[...])
    scores = jax.nn.sigmoid(logits.T[:N_EXPERTS])
    biased = scores + rb_ref[...]
    sel = [biased[e:e + 1] for e in range(N_EXPERTS)]
    sc = [scores[e:e + 1] for e in range(N_EXPERTS)]
    best, best_score = None, None
    for gidx in range(N_GROUPS):
        a, b, c, d = sel[4 * gidx:4 * gidx + 4]
        p, q, r, s = jnp.maximum(a, b), jnp.minimum(a, b), jnp.maximum(c, d), jnp.minimum(c, d)
        gs = jnp.maximum(p, r) + jnp.maximum(jnp.minimum(p, r), jnp.maximum(q, s))
        if gidx == 0:
            best, best_score = jnp.zeros_like(gs, dtype=jnp.int32), gs
        else:
            upd = gs > best_score
            best = jnp.where(upd, gidx, best)
            best_score = jnp.where(upd, gs, best_score)
    pick = lambda rows: [jnp.where(best == 0, rows[j], jnp.where(best == 1, rows[4 + j], jnp.where(
        best == 2, rows[8 + j], rows[12 + j]))) for j in range(EXPERTS_PER_GROUP)]
    v, s4 = pick(sel), pick(sc)
    kept = []
    for j in range(EXPERTS_PER_GROUP):
        beaten = [jnp.where(v[i] >= v[j] if i < j else v[i] > v[j], 1, 0) for i in range(EXPERTS_PER_GROUP) if i != j]
        kept.append(beaten[0] + beaten[1] + beaten[2] < 2)
    zero = jnp.zeros_like(s4[0])
    denom = (jnp.where(kept[0], s4[0], zero) + jnp.where(kept[1], s4[1], zero)
             + jnp.where(kept[2], s4[2], zero) + jnp.where(kept[3], s4[3], zero))
    lo = jnp.where(kept[0], 0, jnp.where(kept[1], 1, 2))
    hi = jnp.where(kept[3], 3, jnp.where(kept[2], 2, 1))
    w_lo = jnp.where(kept[0], s4[0], jnp.where(kept[1], s4[1], s4[2])) / denom
    w_hi = jnp.where(kept[3], s4[3], jnp.where(kept[2], s4[2], s4[1])) / denom
    pair = jnp.where(lo == 0, hi - 1, jnp.where(lo == 1, hi + 1, 5))
    bucket = best * 6 + pair

    brow = lax.broadcasted_iota(jnp.int32, (CNT_ROWS, tm), 0)
    mine = brow == bucket
    onehot = jnp.where(mine, 1.0, 0.0)
    r_i = lax.broadcasted_iota(jnp.int32, (tm, tm), 0)
    c_i = lax.broadcasted_iota(jnp.int32, (tm, tm), 1)
    before = jnp.where(r_i < c_i, 1.0, 0.0).astype(BF16)
    cnt = cnt_ref[...]
    prefix = _dot(onehot.astype(BF16), before) + jnp.concatenate([cnt] * (tm // LANES), axis=1)
    rank = jnp.sum(jnp.where(mine, prefix, 0.0), axis=0, keepdims=True)
    cnt_ref[...] = cnt + jnp.sum(onehot, axis=1, keepdims=True)

    row8 = lax.broadcasted_iota(jnp.int32, (8, tm), 0)
    rec = jnp.where(row8 == 0, w_lo, jnp.where(row8 == 1, w_hi, jnp.where(
        row8 == 2, bucket.astype(F32), jnp.where(row8 == 3, rank, 0.0))))
    rec_ref[...] = rec
    hx_ref[:, D_MODEL:] = jnp.concatenate([rec, jnp.zeros((TAIL - 8, tm), F32)], axis=0).T


def _post_mix(o, w_out, x, ln_g, ln_b, rw_hi, rw_lo, rb_col):
    n, ko = o.shape
    tm = 512
    kern = functools.partial(_post_mix_kernel, tm=tm)
    row = lambda i: (0, 0)
    return pl.pallas_call(
        kern,
        grid=(n // tm,),
        in_specs=[pl.BlockSpec((tm, ko), lambda i: (i, 0)),
                  pl.BlockSpec((ko, D_MODEL), row),
                  pl.BlockSpec((tm, D_MODEL), lambda i: (i, 0)),
                  pl.BlockSpec((1, D_MODEL), row),
                  pl.BlockSpec((1, D_MODEL), row),
                  pl.BlockSpec((D_MODEL, LANES), row),
                  pl.BlockSpec((D_MODEL, LANES), row),
                  pl.BlockSpec((N_EXPERTS, 1), row)],
        out_specs=[pl.BlockSpec((tm, D_MODEL + TAIL), lambda i: (i, 0)),
                   pl.BlockSpec((8, tm), lambda i: (0, i)),
                   pl.BlockSpec((CNT_ROWS, LANES), row)],
        out_shape=[jax.ShapeDtypeStruct((n, D_MODEL + TAIL), F32),
                   jax.ShapeDtypeStruct((8, n), F32),
                   jax.ShapeDtypeStruct((CNT_ROWS, LANES), F32)],
        compiler_params=_cparams("arbitrary"),
        name="post_mix",
    )(o, w_out, x, ln_g, ln_b, rw_hi, rw_lo, rb_col)


PERM_CHUNK = 512
PERM_UNROLL = 8


def _row_copy(src_ref, dst_ref, sem, s, d):
    return pltpu.make_async_copy(src_ref.at[pl.ds(s, 1)], dst_ref.at[pl.ds(d, 1)], sem)


def _permute_rows_loops(copy_row):
    def issue(i, _):
        for u in range(PERM_UNROLL):
            copy_row(i * PERM_UNROLL + u).start()
        return 0

    def drain(i, _):
        for u in range(PERM_UNROLL):
            copy_row(0).wait()
        return 0

    lax.fori_loop(0, PERM_CHUNK // PERM_UNROLL, issue, 0)
    lax.fori_loop(0, PERM_CHUNK // PERM_UNROLL, drain, 0)


def _scatter_rows_kernel(idx_ref, src_ref, init_ref, out_ref, sem):
    del init_ref
    _permute_rows_loops(lambda r: _row_copy(src_ref, out_ref, sem, r, idx_ref[r]))


def _scatter_rows(src, dest, n_out):
    n, w = src.shape
    return pl.pallas_call(
        _scatter_rows_kernel,
        grid=(n // PERM_CHUNK,),
        in_specs=[pl.BlockSpec((PERM_CHUNK,), lambda i: (i,), memory_space=pltpu.SMEM),
                  pl.BlockSpec((PERM_CHUNK, w), lambda i: (i, 0)),
                  pl.BlockSpec(memory_space=pl.ANY)],
        out_specs=pl.BlockSpec(memory_space=pl.ANY),
        out_shape=jax.ShapeDtypeStruct((n_out, w), src.dtype),
        scratch_shapes=[pltpu.SemaphoreType.DMA(())],
        input_output_aliases={2: 0},
        compiler_params=pltpu.CompilerParams(dimension_semantics=("arbitrary",), has_side_effects=True),
        name="scatter_rows",
    )(dest, src, jnp.zeros((n_out, w), src.dtype))


def _gather_rows_kernel(idx_ref, src_ref, out_ref, sem):
    _permute_rows_loops(lambda r: _row_copy(src_ref, out_ref, sem, idx_ref[r], r))


def _gather_rows(src, idx):
    n = idx.shape[0]
    w = src.shape[1]
    return pl.pallas_call(
        _gather_rows_kernel,
        grid=(n // PERM_CHUNK,),
        in_specs=[pl.BlockSpec((PERM_CHUNK,), lambda i: (i,), memory_space=pltpu.SMEM),
                  pl.BlockSpec(memory_space=pl.ANY)],
        out_specs=pl.BlockSpec((PERM_CHUNK, w), lambda i: (i, 0)),
        out_shape=jax.ShapeDtypeStruct((n, w), src.dtype),
        scratch_shapes=[pltpu.SemaphoreType.DMA(())],
        compiler_params=pltpu.CompilerParams(dimension_semantics=("arbitrary",), has_side_effects=True),
        name="gather_rows",
    )(idx, src)


MOE_TILE = 256


def _moe_kernel(src_ref, ea_ref, eb_ref, nused_ref, xs_ref, gua_ref, gub_ref, da_ref, db_ref, y_ref):
    del src_ref, ea_ref, eb_ref

    @pl.when(pl.program_id(0) >= nused_ref[0])
    def _():
        y_ref[...] = jnp.zeros_like(y_ref)

    @pl.when(pl.program_id(0) < nused_ref[0])
    def _():
        x = xs_ref[:, :D_MODEL].astype(BF16)
        tail = xs_ref[:, D_MODEL:]
        lane = lax.broadcasted_iota(jnp.int32, tail.shape, 1)
        w_lo = jnp.sum(jnp.where(lane == 0, tail, 0.0), axis=1, keepdims=True)
        w_hi = jnp.sum(jnp.where(lane == 1, tail, 0.0), axis=1, keepdims=True)

        def expert(gu_ref, d_ref):
            gu = _dot(x, gu_ref[...])
            g_in, u_in = gu[:, :D_EXPERT], gu[:, D_EXPERT:]
            act = (g_in * jax.nn.sigmoid(g_in)) * u_in
            return _dot(act.astype(BF16), d_ref[...])

        y_ref[...] = w_lo * expert(gua_ref, da_ref) + w_hi * expert(gub_ref, db_ref)


def _moe_ffn(xs, tile_src, tile_ea, tile_eb, n_used, w_gate_up, w_down):
    n_pad = xs.shape[0]
    n_tiles = n_pad // MOE_TILE
    grid_spec = pltpu.PrefetchScalarGridSpec(
        num_scalar_prefetch=4,
        grid=(n_tiles,),
        in_specs=[pl.BlockSpec((MOE_TILE, D_MODEL + TAIL), lambda t, src, ea, eb, nu: (src[t], 0)),
                  pl.BlockSpec((None, D_MODEL, 2 * D_EXPERT), lambda t, src, ea, eb, nu: (ea[t], 0, 0)),
                  pl.BlockSpec((None, D_MODEL, 2 * D_EXPERT), lambda t, src, ea, eb, nu: (eb[t], 0, 0)),
                  pl.BlockSpec((None, D_EXPERT, D_MODEL), lambda t, src, ea, eb, nu: (ea[t], 0, 0)),
                  pl.BlockSpec((None, D_EXPERT, D_MODEL), lambda t, src, ea, eb, nu: (eb[t], 0, 0))],
        out_specs=pl.BlockSpec((MOE_TILE, D_MODEL), lambda t, src, ea, eb, nu: (t, 0)),
    )
    return pl.pallas_call(
        _moe_kernel,
        grid_spec=grid_spec,
        out_shape=jax.ShapeDtypeStruct((n_pad, D_MODEL), F32),
        compiler_params=_cparams("arbitrary"),
        name="moe_ffn",
    )(tile_src, tile_ea, tile_eb, n_used, xs, w_gate_up, w_gate_up, w_down, w_down)


def _routing_plan(rec, counts, n):
    n_tiles = n // MOE_TILE + N_BUCKETS
    cnt = counts[:N_BUCKETS, 0].astype(jnp.int32)
    tiles_per = (cnt + MOE_TILE - 1) // MOE_TILE
    tile_end = jnp.cumsum(tiles_per)
    tile_start = tile_end - tiles_per
    bucket = rec[2].astype(jnp.int32)
    rank = rec[3].astype(jnp.int32)
    dest = tile_start[bucket] * MOE_TILE + rank
    n_used = tile_end[-1]
    tile_id = jnp.clip(jnp.arange(n_tiles, dtype=jnp.int32), 0, jnp.maximum(n_used - 1, 0))
    tile_bucket = jnp.sum(tile_id[:, None] >= tile_end[None, :], axis=1).astype(jnp.int32)
    group = tile_bucket // 6
    pair = tile_bucket % 6
    lo = jnp.where(pair < 3, 0, jnp.where(pair < 5, 1, 2))
    hi = jnp.where(pair < 3, pair + 1, jnp.where(pair < 5, pair - 1, 3))
    return dest, tile_id, group * 4 + lo, group * 4 + hi, n_used.reshape(1).astype(jnp.int32), n_tiles * MOE_TILE


def _final_kernel(hx_ref, moe_ref, p_ref, wg_ref, wp_ref, g_ref, b_ref, o_ref):
    h = hx_ref[...]
    gate = jax.nn.sigmoid(_dot(h.astype(BF16), wg_ref[...]))
    ple = gate * _dot(p_ref[...].astype(BF16), wp_ref[...])
    o_ref[...] = _layer_norm(DEEPNORM_ALPHA * h + moe_ref[...] + ple, g_ref[...], b_ref[...])


def _final(hx, moe, p, w_gate, w_proj, ln_g, ln_b):
    n = moe.shape[0]
    tm = 512
    row = lambda i: (0, 0)
    return pl.pallas_call(
        _final_kernel,
        grid=(n // tm,),
        in_specs=[pl.BlockSpec((tm, D_MODEL), lambda i: (i, 0)),
                  pl.BlockSpec((tm, D_MODEL), lambda i: (i, 0)),
                  pl.BlockSpec((tm, PLE_DIM), lambda i: (i, 0)),
                  pl.BlockSpec((D_MODEL, D_MODEL), row),
                  pl.BlockSpec((PLE_DIM, D_MODEL), row),
                  pl.BlockSpec((1, D_MODEL), row),
                  pl.BlockSpec((1, D_MODEL), row)],
        out_specs=pl.BlockSpec((tm, D_MODEL), lambda i: (i, 0)),
        out_shape=jax.ShapeDtypeStruct((n, D_MODEL), F32),
        compiler_params=_cparams("parallel"),
        name="final",
    )(hx, moe, p, w_gate, w_proj, ln_g, ln_b)


def _pad_lanes(a, width=LANES):
    return jnp.pad(a, [(0, 0)] * (a.ndim - 1) + [(0, width - a.shape[-1])])


def _moe_and_norm(o, w_out, x, p_i, ln1_g, ln1_b, router, w_gate_up, w_down, ple_w_gate, ple_w_proj,
                  ln2_g, ln2_b):
    n = x.shape[0]
    hx, rec, counts = _post_mix(o, w_out.astype(BF16), x, ln1_g[None], ln1_b[None], *router)
    dest, tile_src, tile_ea, tile_eb, n_used, n_pad = _routing_plan(rec, counts, n)
    xs = _scatter_rows(hx, dest, n_pad)
    ys = _moe_ffn(xs, tile_src, tile_ea, tile_eb, n_used, w_gate_up.astype(BF16), w_down.astype(BF16))
    moe = _gather_rows(ys, dest)
    return _final(hx, moe, p_i, ple_w_gate.astype(BF16), ple_w_proj.astype(BF16), ln2_g[None], ln2_b[None])


def kernel(x, p, sb_w_in, sb_w_out, gdn_w_in, gdn_conv_w, gdn_a_log, gdn_dt_bias, gdn_norm_w, gdn_w_out,
           ln1_g, ln1_b, router_w, router_b, expert_w_gate_up, expert_w_down, ple_w_gate, ple_w_proj,
           ln2_g, ln2_b):
    bsz, t, d = x.shape
    n = bsz * t
    x = x.reshape(n, d)
    p = p.reshape(DEPTH, n, PLE_DIM)
    rw_pad = _pad_lanes(router_w)
    rw_hi = rw_pad.astype(BF16)
    router = (rw_hi, (rw_pad - rw_hi.astype(F32)).astype(BF16), router_b[:, None])

    qkv = _linear(x, sb_w_in[0].astype(BF16), BF16, 1024, 1024)
    o = _sb_attention(qkv.reshape(bsz, t, 3 * d), bsz, t).reshape(n, d)
    x = _moe_and_norm(o, sb_w_out[0], x, p[0], ln1_g[0], ln1_b[0], router, expert_w_gate_up[0],
                      expert_w_down[0], ple_w_gate[0], ple_w_proj[0], ln2_g[0], ln2_b[0])

    w_in = gdn_w_in[0]
    split = GDN_QKV_DIM + GDN_VAL_DIM
    proj = _linear(x, w_in[:, :split].astype(BF16), BF16, 1024, 1024)
    ba = _linear(x, _pad_lanes(w_in[:, split:]).astype(BF16), F32, 1024, LANES)
    proj = proj.reshape(bsz, t, split)
    qkv_c, kt = _gdn_conv(proj, gdn_conv_w[0], bsz, t)
    head_pad = lambda a: jnp.pad(a, (GDN_V_HEADS, LANES - 2 * GDN_V_HEADS))[None]
    gb, gbt = _gdn_gate_prep(ba.reshape(bsz, t, LANES), head_pad(gdn_a_log[0]), head_pad(gdn_dt_bias[0]), bsz, t)
    o = _gdn_recurrence(qkv_c, kt, proj, gb, gbt, gdn_norm_w[0][None], bsz, t).reshape(n, GDN_VAL_DIM)
    x = _moe_and_norm(o, gdn_w_out[0], x, p[1], ln1_g[1], ln1_b[1], router, expert_w_gate_up[1],
                      expert_w_down[1], ple_w_gate[1], ple_w_proj[1], ln2_g[1], ln2_b[1])
    return x.reshape(bsz, t, d)
```

```python
import functools

import jax
import jax.numpy as jnp
from jax import lax
from jax.experimental import pallas as pl
from jax.experimental.pallas import tpu as pltpu

D_MODEL = 1024
DEPTH = 2
PLE_DIM = 256
SB_HEADS = 16
SB_HEAD_DIM = 64
GDN_QK_HEADS = 8
GDN_V_HEADS = 16
GDN_HEAD_DIM = 128
GDN_KEY_DIM = GDN_QK_HEADS * GDN_HEAD_DIM
GDN_VAL_DIM = GDN_V_HEADS * GDN_HEAD_DIM
GDN_QKV_DIM = 2 * GDN_KEY_DIM + GDN_VAL_DIM
CONV_K = 4
CHUNK = 64
N_EXPERTS = 16
N_GROUPS = 4
EXPERTS_PER_GROUP = 4
D_EXPERT = 512
N_BUCKETS = 24
LN_EPS = 1e-5
RMS_EPS = 1e-6
DEEPNORM_ALPHA = (2 * DEPTH) ** 0.25

LANES = 128
TAIL = LANES
VMEM_LIMIT = 48 * 1024 * 1024
SB_DEAD_LOG2 = -151.0
LOG2_E = 1.4426950408889634

F32 = jnp.float32
BF16 = jnp.bfloat16


def _cparams(*sem):
    return pltpu.CompilerParams(dimension_semantics=sem, vmem_limit_bytes=VMEM_LIMIT)


def _dot(a, b):
    return jnp.dot(a, b, preferred_element_type=F32)


def _dot_nt(a, b):
    return lax.dot_general(a, b, (((1,), (1,)), ((), ())), preferred_element_type=F32)


def _split_dot(a, b_bf16, parts):
    acc = None
    rem = a
    for _ in range(parts):
        hi = rem.astype(BF16)
        term = _dot(hi, b_bf16)
        acc = term if acc is None else acc + term
        rem = rem - hi.astype(F32)
    return acc


def _linear_kernel(x_ref, w_ref, o_ref):
    o_ref[...] = _dot(x_ref[...].astype(BF16), w_ref[...]).astype(o_ref.dtype)


def _linear(x, w, out_dtype, tm, tn):
    m, k = x.shape
    n = w.shape[1]
    return pl.pallas_call(
        _linear_kernel,
        grid=(m // tm, n // tn),
        in_specs=[pl.BlockSpec((tm, k), lambda i, j: (i, 0)),
                  pl.BlockSpec((k, tn), lambda i, j: (0, j))],
        out_specs=pl.BlockSpec((tm, tn), lambda i, j: (i, j)),
        out_shape=jax.ShapeDtypeStruct((m, n), out_dtype),
        compiler_params=_cparams("parallel", "parallel"),
        name="linear",
    )(x, w)


SB_TQ = 128
SB_TK = 128
SB_GROUP = 8
SB_PIECE = 64


def _sb_kernel(q_ref, k_ref, v_ref, o_ref, carry_ref, acc_ref, *, scale):
    t = q_ref.shape[0]
    tq, tk, grp = SB_TQ, SB_TK, SB_GROUP
    rows = 2 * tq
    nrow = grp * rows
    npiece = nrow // SB_PIECE
    lane = lax.broadcasted_iota(jnp.int32, (tq, LANES), 1)
    first_head = lane < SB_HEAD_DIM
    jj = lax.broadcasted_iota(jnp.int32, (2 * tk, 2 * tk), 0) % tk
    ss = lax.broadcasted_iota(jnp.int32, (2 * tk, 2 * tk), 1)
    cum_mat = jnp.where((jj > ss) | (ss >= tk), 1.0, 0.0).astype(BF16)
    rp = lax.broadcasted_iota(jnp.int32, (SB_PIECE, tk), 0)
    cp = lax.broadcasted_iota(jnp.int32, (SB_PIECE, tk), 1)
    pieces_per_half = tq // SB_PIECE
    diag_mask = [cp < rp + (p % pieces_per_half) * SB_PIECE for p in range(pieces_per_half)]
    piece = lambda a, p: a[p * SB_PIECE:(p + 1) * SB_PIECE]

    def group(gi, _):
        first_tile = gi * grp
        qs = []
        for a in range(grp):
            q = q_ref[pl.ds(pl.multiple_of((first_tile + a) * tq, tq), tq), :] * scale
            zero = jnp.zeros_like(q)
            qs += [jnp.where(first_head, q, zero), jnp.where(first_head, zero, q)]
        qq = jnp.concatenate(qs, axis=0).reshape(grp, rows, LANES)
        carry_ref[...] = jnp.zeros_like(carry_ref)
        acc_ref[...] = jnp.zeros_like(acc_ref)

        def alive(c):
            d, top = c
            return jnp.logical_and(d < first_tile + grp, top >= SB_DEAD_LOG2)

        def step(d, diagonal):
            ks, vs = [], []
            for a in range(grp):
                k0 = pl.multiple_of(jnp.maximum(first_tile + a - d, 0) * tk, tk)
                ks.append(k_ref[pl.ds(k0, tk), :])
                vs.append(v_ref[pl.ds(k0, tk), :])
            z = jnp.einsum("gqd,gkd->gqk", qq, jnp.stack(ks), preferred_element_type=F32).reshape(nrow, tk)
            hls, lps = [], []
            for p in range(npiece):
                z2 = piece(z, p) * LOG2_E
                lb = jnp.minimum(z2, 0.0) - jnp.log2(1.0 + jnp.exp2(-jnp.abs(z2)))
                lm = lb - z2
                if diagonal:
                    mp = diag_mask[p % pieces_per_half]
                else:
                    mp = first_tile + p // (2 * pieces_per_half) - d >= 0
                lm = jnp.where(mp, lm, 0.0)
                hi = lm.astype(BF16)
                hls.append(jnp.concatenate([hi, (lm - hi.astype(F32)).astype(BF16)], axis=1))
                lps.append(jnp.where(mp, lb, -1e30))
            cs = _dot(jnp.concatenate(hls, axis=0), cum_mat)
            attn, top = [], None
            for p in range(npiece):
                rr = pl.ds(p * SB_PIECE, SB_PIECE)
                csp = piece(cs, p)
                carry = carry_ref[rr, :]
                attn.append(jnp.exp2(lps[p] + csp[:, :tk] + carry).astype(BF16))
                carry = carry + csp[:, tk:]
                carry_ref[rr, :] = carry
                top = carry if top is None else jnp.maximum(top, carry)
            av = jnp.einsum("gqk,gkd->gqd", jnp.concatenate(attn, axis=0).reshape(grp, rows, tk), jnp.stack(vs),
                            preferred_element_type=F32)
            acc_ref[...] += av.reshape(nrow, LANES)
            return d + 1, jnp.max(top)

        lax.while_loop(alive, lambda c: step(c[0], False), step(jnp.int32(0), True))
        for a in range(grp):
            acc = acc_ref[a * rows:(a + 1) * rows, :]
            o_ref[pl.ds(pl.multiple_of((first_tile + a) * tq, tq), tq), :] = jnp.where(
                first_head, acc[:tq], acc[tq:]).astype(o_ref.dtype)
        return 0

    lax.fori_loop(0, t // (tq * grp), group, 0)


def _sb_attention(qkv, bsz, t):
    ncol = D_MODEL // LANES
    nrow = SB_GROUP * 2 * SB_TQ
    kern = functools.partial(_sb_kernel, scale=SB_HEAD_DIM ** -0.5)
    return pl.pallas_call(
        kern,
        grid=(bsz, ncol),
        in_specs=[pl.BlockSpec((None, t, LANES), lambda b, h: (b, 0, h)),
                  pl.BlockSpec((None, t, LANES), lambda b, h: (b, 0, ncol + h)),
                  pl.BlockSpec((None, t, LANES), lambda b, h: (b, 0, 2 * ncol + h))],
        out_specs=pl.BlockSpec((None, t, LANES), lambda b, h: (b, 0, h)),
        out_shape=jax.ShapeDtypeStruct((bsz, t, D_MODEL), BF16),
        scratch_shapes=[pltpu.VMEM((nrow, SB_TK), F32), pltpu.VMEM((nrow, LANES), F32)],
        compiler_params=_cparams("parallel", "parallel"),
        name="sb_attention",
    )(qkv, qkv, qkv)


CONV_PIECE = 128


def _conv_kernel(x_ref, halo_ref, w_ref, o_ref, kt_ref, *, tt, tc):
    i = pl.program_id(1)
    j = pl.program_id(2)
    w = w_ref[...]
    q_blocks = GDN_KEY_DIM // tc

    def run(normalise, transpose):
        for pc in range(tt // CONV_PIECE):
            rows = slice(pc * CONV_PIECE, (pc + 1) * CONV_PIECE)
            if pc == 0:
                prev = jnp.where(i > 0, halo_ref[...].astype(F32), 0.0)
            else:
                prev = x_ref[pc * CONV_PIECE - 8:pc * CONV_PIECE, :].astype(F32)
            cur = x_ref[rows, :].astype(F32)
            ext = jnp.concatenate([prev, cur], axis=0)
            acc = cur * w[CONV_K - 1:CONV_K, :]
            for s in range(1, CONV_K):
                acc = acc + pltpu.roll(ext, s, 0)[8:] * w[CONV_K - 1 - s:CONV_K - s, :]
            y = acc * jax.nn.sigmoid(acc)
            if normalise:
                parts = []
                for h in range(tc // GDN_HEAD_DIM):
                    yh = y[:, h * GDN_HEAD_DIM:(h + 1) * GDN_HEAD_DIM]
                    parts.append(yh * lax.rsqrt(jnp.sum(yh * yh, axis=1, keepdims=True) + RMS_EPS))
                y = jnp.concatenate(parts, axis=1)
            o_ref[rows, :] = y.astype(o_ref.dtype)
            if transpose:
                kt_ref[:, rows] = y.T.astype(kt_ref.dtype)

    @pl.when(j < q_blocks)
    def _():
        run(True, False)

    @pl.when((j >= q_blocks) & (j < 2 * q_blocks))
    def _():
        run(True, True)

    @pl.when(j >= 2 * q_blocks)
    def _():
        run(False, False)


def _gdn_conv(proj, conv_w, bsz, t):
    tt, tc = 2048, 256
    kern = functools.partial(_conv_kernel, tt=tt, tc=tc)
    q_blocks = GDN_KEY_DIM // tc
    return pl.pallas_call(
        kern,
        grid=(bsz, t // tt, GDN_QKV_DIM // tc),
        in_specs=[pl.BlockSpec((None, tt, tc), lambda b, i, j: (b, i, j)),
                  pl.BlockSpec((None, 8, tc), lambda b, i, j: (b, jnp.maximum(i * (tt // 8) - 1, 0), j)),
                  pl.BlockSpec((CONV_K, tc), lambda b, i, j: (0, j))],
        out_specs=[pl.BlockSpec((None, tt, tc), lambda b, i, j: (b, i, j)),
                   pl.BlockSpec((None, tc, tt), lambda b, i, j: (b, jnp.clip(j - q_blocks, 0, q_blocks - 1), i))],
        out_shape=[jax.ShapeDtypeStruct((bsz, t, GDN_QKV_DIM), BF16),
                   jax.ShapeDtypeStruct((bsz, GDN_KEY_DIM, t), BF16)],
        compiler_params=_cparams("parallel", "parallel", "arbitrary"),
        name="gdn_conv",
    )(proj, proj, conv_w)


def _gate_prep_kernel(ba_ref, alog_ref, dtb_ref, gb_ref, gbt_ref, *, tt):
    ba = ba_ref[...]
    lane = lax.broadcasted_iota(jnp.int32, (tt, LANES), 1)
    beta = jax.nn.sigmoid(ba)
    xa = ba + dtb_ref[...]
    softplus = jnp.maximum(xa, 0.0) + jnp.log1p(jnp.exp(-jnp.abs(xa)))
    g = -jnp.exp(alog_ref[...]) * softplus
    r = lax.broadcasted_iota(jnp.int32, (tt, tt), 0)
    c = lax.broadcasted_iota(jnp.int32, (tt, tt), 1)
    tri = jnp.where((c <= r) & (c // CHUNK == r // CHUNK), 1.0, 0.0).astype(BF16)
    is_g = (lane >= GDN_V_HEADS) & (lane < 2 * GDN_V_HEADS)
    g = jnp.where(is_g, g, 0.0)
    rem = g
    gc = jnp.zeros_like(g)
    for _ in range(3):
        hi = rem.astype(BF16)
        gc = gc + _dot(tri, hi)
        rem = rem - hi.astype(F32)
    out = jnp.where(lane < GDN_V_HEADS, beta, gc)
    gb_ref[...] = out
    gbt_ref[...] = out.T


def _gdn_gate_prep(ba, a_log_row, dt_bias_row, bsz, t):
    tt = 512
    kern = functools.partial(_gate_prep_kernel, tt=tt)
    return pl.pallas_call(
        kern,
        grid=(bsz, t // tt),
        in_specs=[pl.BlockSpec((None, tt, LANES), lambda b, i: (b, i, 0)),
                  pl.BlockSpec((1, LANES), lambda b, i: (0, 0)),
                  pl.BlockSpec((1, LANES), lambda b, i: (0, 0))],
        out_specs=[pl.BlockSpec((None, tt, LANES), lambda b, i: (b, i, 0)),
                   pl.BlockSpec((None, LANES, tt), lambda b, i: (b, 0, i))],
        out_shape=[jax.ShapeDtypeStruct((bsz, t, LANES), F32),
                   jax.ShapeDtypeStruct((bsz, LANES, t), F32)],
        compiler_params=_cparams("parallel", "parallel"),
        name="gdn_gate_prep",
    )(ba, a_log_row, dt_bias_row)


GDN_TT = 512
GDN_HG = 16
GDN_PA = 4


def _bmm(a, b):
    return jnp.einsum("cik,ckj->cij", a, b, preferred_element_type=F32)


def _gdn_kernel(q_ref, k_ref, kt_ref, v_ref, z_ref, gb_ref, gbt_ref, nw_ref, o_ref,
                s_ref, u_ref, w_ref, at_ref, qg_ref, ktd_ref, el_ref):
    tt, hg_n = GDN_TT, GDN_HG
    nc = tt // CHUNK
    hg = pl.program_id(1)

    @pl.when(pl.program_id(2) == 0)
    def _():
        s_ref[...] = jnp.zeros_like(s_ref)

    lane = lax.broadcasted_iota(jnp.int32, (tt, LANES), 1)
    sub8 = lax.broadcasted_iota(jnp.int32, (hg_n, tt), 0)
    ri = lax.broadcasted_iota(jnp.int32, (1, CHUNK, CHUNK), 1)
    ci = lax.broadcasted_iota(jnp.int32, (1, CHUNK, CHUNK), 2)
    lower_incl = ri >= ci
    strict = ri > ci
    qscale = GDN_HEAD_DIM ** -0.5
    gb = gb_ref[...]
    rows8 = gbt_ref[pl.ds(pl.multiple_of(GDN_V_HEADS + hg * hg_n, 8), hg_n), :]
    chunks = lambda a: a.reshape(nc, CHUNK, a.shape[-1])
    lane_chunks = lambda a: jnp.stack([a[:, c * CHUNK:(c + 1) * CHUNK] for c in range(nc)])

    cat = lambda xs: jnp.concatenate(xs, axis=0)
    for s0 in range(0, hg_n, GDN_PA):
        heads = range(s0, s0 + GDN_PA)
        q3s, k3s, kt3s, kks, qks = {}, {}, {}, {}, {}
        for qh in sorted({s // 2 for s in heads}):
            cols = slice(qh * LANES, (qh + 1) * LANES)
            q3s[qh] = chunks(q_ref[:, cols].astype(F32) * qscale)
            k3s[qh] = chunks(k_ref[:, cols].astype(F32))
            kt3s[qh] = lane_chunks(kt_ref[cols, :])
            kks[qh] = _bmm(k3s[qh].astype(BF16), kt3s[qh])
            qks[qh] = _bmm(q3s[qh].astype(BF16), kt3s[qh])
        per_head = lambda d: cat([d[s // 2] for s in heads])
        q3, k3, kt3, kk, qk = per_head(q3s), per_head(k3s), per_head(kt3s), per_head(kks), per_head(qks)
        bcol = cat([chunks(jnp.sum(jnp.where(lane == hg * hg_n + s, gb, 0.0), axis=1, keepdims=True)) for s in heads])
        gcol = cat([chunks(jnp.sum(jnp.where(lane == GDN_V_HEADS + hg * hg_n + s, gb, 0.0), axis=1, keepdims=True))
                    for s in heads])
        grow = cat([lane_chunks(jnp.sum(jnp.where(sub8 == s, rows8, 0.0), axis=0, keepdims=True))
                    for s in heads])
        v3 = cat([chunks(v_ref[:, s * LANES:(s + 1) * LANES].astype(F32)) for s in heads])
        decay = jnp.where(lower_incl, jnp.exp(jnp.minimum(gcol - grow, 0.0)), 0.0)
        a = jnp.where(strict, bcol * kk * decay, 0.0)
        eg = jnp.exp(gcol)
        p = a.astype(BF16)
        t_inv = jnp.where(ri == ci, 1.0, 0.0) - a
        for _ in range(5):
            p = _bmm(p, p).astype(BF16)
            t_inv = t_inv + _bmm(t_inv.astype(BF16), p)
        rhs = _bmm(t_inv.astype(BF16), jnp.concatenate([v3 * bcol, k3 * (bcol * eg)], axis=2).astype(BF16))
        g_last = gcol[:, CHUNK - 1:CHUNK, :]
        hs = slice(s0, s0 + GDN_PA)
        u_ref[hs] = rhs[:, :, :LANES].reshape(GDN_PA, tt, LANES)
        w_ref[hs] = rhs[:, :, LANES:].astype(BF16).reshape(GDN_PA, tt, LANES)
        at_ref[hs] = (qk * decay).astype(BF16).reshape(GDN_PA, tt, CHUNK)
        qg_ref[hs] = (q3 * eg).astype(BF16).reshape(GDN_PA, tt, LANES)
        ktd_ref[hs] = (kt3.astype(F32) * jnp.exp(g_last - grow)).astype(BF16).reshape(
            GDN_PA, nc, GDN_HEAD_DIM, CHUNK)
        el_ref[hs] = jnp.broadcast_to(jnp.exp(g_last), (GDN_PA * nc, 1, LANES)).reshape(GDN_PA, nc, 1, LANES)

    nw = nw_ref[...]

    def chunk(c, _):
        rr = pl.ds(pl.multiple_of(c * CHUNK, CHUNK), CHUNK)
        st = s_ref[...]
        st_bf = st.astype(BF16)
        v_new = u_ref[:, rr, :] - _bmm(w_ref[:, rr, :], st_bf)
        v_new_bf = v_new.astype(BF16)
        o_c = _bmm(qg_ref[:, rr, :], st_bf) + _bmm(at_ref[:, rr, :], v_new_bf)
        s_ref[...] = st * el_ref[:, c] + _bmm(ktd_ref[:, c], v_new_bf)
        o_n = o_c * lax.rsqrt(jnp.mean(o_c * o_c, axis=2, keepdims=True) + RMS_EPS) * nw
        zz = z_ref[rr, :].astype(F32)
        for s in range(hg_n):
            zs = zz[:, s * LANES:(s + 1) * LANES]
            o_ref[rr, s * LANES:(s + 1) * LANES] = (o_n[s] * (zs * jax.nn.sigmoid(zs))).astype(o_ref.dtype)
        return 0

    lax.fori_loop(0, nc, chunk, 0)


def _gdn_recurrence(qkv, kt, proj, gb, gbt, norm_w_row, bsz, t):
    tt, hg_n = GDN_TT, GDN_HG
    nc = tt // CHUNK
    qw = (hg_n // 2) * LANES
    vw = hg_n * LANES
    q_blk, v_blk, z_blk = GDN_KEY_DIM // qw, 2 * GDN_KEY_DIM // vw, GDN_QKV_DIM // vw
    return pl.pallas_call(
        _gdn_kernel,
        grid=(bsz, GDN_V_HEADS // hg_n, t // tt),
        in_specs=[pl.BlockSpec((None, tt, qw), lambda b, j, i: (b, i, j)),
                  pl.BlockSpec((None, tt, qw), lambda b, j, i: (b, i, q_blk + j)),
                  pl.BlockSpec((None, qw, tt), lambda b, j, i: (b, j, i)),
                  pl.BlockSpec((None, tt, vw), lambda b, j, i: (b, i, v_blk + j)),
                  pl.BlockSpec((None, tt, vw), lambda b, j, i: (b, i, z_blk + j)),
                  pl.BlockSpec((None, tt, LANES), lambda b, j, i: (b, i, 0)),
                  pl.BlockSpec((None, LANES, tt), lambda b, j, i: (b, 0, i)),
                  pl.BlockSpec((1, LANES), lambda b, j, i: (0, 0))],
        out_specs=pl.BlockSpec((None, tt, vw), lambda b, j, i: (b, i, j)),
        out_shape=jax.ShapeDtypeStruct((bsz, t, GDN_VAL_DIM), BF16),
        scratch_shapes=[pltpu.VMEM((hg_n, GDN_HEAD_DIM, GDN_HEAD_DIM), F32),
                        pltpu.VMEM((hg_n, tt, LANES), F32),
                        pltpu.VMEM((hg_n, tt, LANES), BF16),
                        pltpu.VMEM((hg_n, tt, CHUNK), BF16),
                        pltpu.VMEM((hg_n, tt, LANES), BF16),
                        pltpu.VMEM((hg_n, nc, GDN_HEAD_DIM, CHUNK), BF16),
                        pltpu.VMEM((hg_n, nc, 1, LANES), F32)],
        compiler_params=_cparams("parallel", "parallel", "arbitrary"),
        name="gdn_recurrence",
    )(qkv, qkv, kt, qkv, proj, gb, gbt, norm_w_row)


def _layer_norm(r, g, b):
    mu = jnp.mean(r, axis=1, keepdims=True)
    d = r - mu
    var = jnp.mean(d * d, axis=1, keepdims=True)
    return d * lax.rsqrt(var + LN_EPS) * g + b


CNT_ROWS = 32


def _post_mix_kernel(o_ref, w_ref, x_ref, g_ref, b_ref, rwh_ref, rwl_ref, rb_ref, hx_ref, rec_ref, cnt_ref, *, tm):
    @pl.when(pl.program_id(0) == 0)
    def _():
        cnt_ref[...] = jnp.zeros_like(cnt_ref)

    mix = _dot(o_ref[...], w_ref[...])
    h = _layer_norm(DEEPNORM_ALPHA * x_ref[...] + mix, g_ref[...], b_ref[...])
    hx_ref[:, :D_MODEL] = h

    h_hi = h.astype(BF16)
    h_lo = (h - h_hi.astype(F32)).astype(BF16)
    logits = _dot(h_hi, rwh_ref[...]) + _dot(h_lo, rwh_ref[...]) + _dot(h_hi, rwl_ref[...])
    scores = jax.nn.sigmoid(logits.T[:N_EXPERTS])
    biased = scores + rb_ref[...]
    sel = [biased[e:e + 1] for e in range(N_EXPERTS)]
    sc = [scores[e:e + 1] for e in range(N_EXPERTS)]
    best, best_score = None, None
    for gidx in range(N_GROUPS):
        a, b, c, d = sel[4 * gidx:4 * gidx + 4]
        p, q, r, s = jnp.maximum(a, b), jnp.minimum(a, b), jnp.maximum(c, d), jnp.minimum(c, d)
        gs = jnp.maximum(p, r) + jnp.maximum(jnp.minimum(p, r), jnp.maximum(q, s))
        if gidx == 0:
            best, best_score = jnp.zeros_like(gs, dtype=jnp.int32), gs
        else:
            upd = gs > best_score
            best = jnp.where(upd, gidx, best)
            best_score = jnp.where(upd, gs, best_score)
    pick = lambda rows: [jnp.where(best == 0, rows[j], jnp.where(best == 1, rows[4 + j], jnp.where(
        best == 2, rows[8 + j], rows[12 + j]))) for j in range(EXPERTS_PER_GROUP)]
    v, s4 = pick(sel), pick(sc)
    kept = []
    for j in range(EXPERTS_PER_GROUP):
        beaten = [jnp.where(v[i] >= v[j] if i < j else v[i] > v[j], 1, 0) for i in range(EXPERTS_PER_GROUP) if i != j]
        kept.append(beaten[0] + beaten[1] + beaten[2] < 2)
    zero = jnp.zeros_like(s4[0])
    denom = (jnp.where(kept[0], s4[0], zero) + jnp.where(kept[1], s4[1], zero)
             + jnp.where(kept[2], s4[2], zero) + jnp.where(kept[3], s4[3], zero))
    lo = jnp.where(kept[0], 0, jnp.where(kept[1], 1, 2))
    hi = jnp.where(kept[3], 3, jnp.where(kept[2], 2, 1))
    w_lo = jnp.where(kept[0], s4[0], jnp.where(kept[1], s4[1], s4[2])) / denom
    w_hi = jnp.where(kept[3], s4[3], jnp.where(kept[2], s4[2], s4[1])) / denom
    pair = jnp.where(lo == 0, hi - 1, jnp.where(lo == 1, hi + 1, 5))
    bucket = best * 6 + pair

    brow = lax.broadcasted_iota(jnp.int32, (CNT_ROWS, tm), 0)
    mine = brow == bucket
    onehot = jnp.where(mine, 1.0, 0.0)
    r_i = lax.broadcasted_iota(jnp.int32, (tm, tm), 0)
    c_i = lax.broadcasted_iota(jnp.int32, (tm, tm), 1)
    before = jnp.where(r_i < c_i, 1.0, 0.0).astype(BF16)
    cnt = cnt_ref[...]
    prefix = _dot(onehot.astype(BF16), before) + jnp.concatenate([cnt] * (tm // LANES), axis=1)
    rank = jnp.sum(jnp.where(mine, prefix, 0.0), axis=0, keepdims=True)
    cnt_ref[...] = cnt + jnp.sum(onehot, axis=1, keepdims=True)

    row8 = lax.broadcasted_iota(jnp.int32, (8, tm), 0)
    rec = jnp.where(row8 == 0, w_lo, jnp.where(row8 == 1, w_hi, jnp.where(
        row8 == 2, bucket.astype(F32), jnp.where(row8 == 3, rank, 0.0))))
    rec_ref[...] = rec
    hx_ref[:, D_MODEL:] = jnp.concatenate([rec, jnp.zeros((TAIL - 8, tm), F32)], axis=0).T


def _post_mix(o, w_out, x, ln_g, ln_b, rw_hi, rw_lo, rb_col):
    n, ko = o.shape
    tm = 512
    kern = functools.partial(_post_mix_kernel, tm=tm)
    row = lambda i: (0, 0)
    return pl.pallas_call(
        kern,
        grid=(n // tm,),
        in_specs=[pl.BlockSpec((tm, ko), lambda i: (i, 0)),
                  pl.BlockSpec((ko, D_MODEL), row),
                  pl.BlockSpec((tm, D_MODEL), lambda i: (i, 0)),
                  pl.BlockSpec((1, D_MODEL), row),
                  pl.BlockSpec((1, D_MODEL), row),
                  pl.BlockSpec((D_MODEL, LANES), row),
                  pl.BlockSpec((D_MODEL, LANES), row),
                  pl.BlockSpec((N_EXPERTS, 1), row)],
        out_specs=[pl.BlockSpec((tm, D_MODEL + TAIL), lambda i: (i, 0)),
                   pl.BlockSpec((8, tm), lambda i: (0, i)),
                   pl.BlockSpec((CNT_ROWS, LANES), row)],
        out_shape=[jax.ShapeDtypeStruct((n, D_MODEL + TAIL), F32),
                   jax.ShapeDtypeStruct((8, n), F32),
                   jax.ShapeDtypeStruct((CNT_ROWS, LANES), F32)],
        compiler_params=_cparams("arbitrary"),
        name="post_mix",
    )(o, w_out, x, ln_g, ln_b, rw_hi, rw_lo, rb_col)


PERM_CHUNK = 512
PERM_UNROLL = 8


def _row_copy(src_ref, dst_ref, sem, s, d):
    return pltpu.make_async_copy(src_ref.at[pl.ds(s, 1)], dst_ref.at[pl.ds(d, 1)], sem)


def _permute_rows_loops(copy_row):
    def issue(i, _):
        for u in range(PERM_UNROLL):
            copy_row(i * PERM_UNROLL + u).start()
        return 0

    def drain(i, _):
        for u in range(PERM_UNROLL):
            copy_row(0).wait()
        return 0

    lax.fori_loop(0, PERM_CHUNK // PERM_UNROLL, issue, 0)
    lax.fori_loop(0, PERM_CHUNK // PERM_UNROLL, drain, 0)


def _scatter_rows_kernel(idx_ref, src_ref, init_ref, out_ref, sem):
    del init_ref
    _permute_rows_loops(lambda r: _row_copy(src_ref, out_ref, sem, r, idx_ref[r]))


def _scatter_rows(src, dest, n_out):
    n, w = src.shape
    return pl.pallas_call(
        _scatter_rows_kernel,
        grid=(n // PERM_CHUNK,),
        in_specs=[pl.BlockSpec((PERM_CHUNK,), lambda i: (i,), memory_space=pltpu.SMEM),
                  pl.BlockSpec((PERM_CHUNK, w), lambda i: (i, 0)),
                  pl.BlockSpec(memory_space=pl.ANY)],
        out_specs=pl.BlockSpec(memory_space=pl.ANY),
        out_shape=jax.ShapeDtypeStruct((n_out, w), src.dtype),
        scratch_shapes=[pltpu.SemaphoreType.DMA(())],
        input_output_aliases={2: 0},
        compiler_params=pltpu.CompilerParams(dimension_semantics=("arbitrary",), has_side_effects=True),
        name="scatter_rows",
    )(dest, src, jnp.zeros((n_out, w), src.dtype))


def _gather_rows_kernel(idx_ref, src_ref, out_ref, sem):
    _permute_rows_loops(lambda r: _row_copy(src_ref, out_ref, sem, idx_ref[r], r))


def _gather_rows(src, idx):
    n = idx.shape[0]
    w = src.shape[1]
    return pl.pallas_call(
        _gather_rows_kernel,
        grid=(n // PERM_CHUNK,),
        in_specs=[pl.BlockSpec((PERM_CHUNK,), lambda i: (i,), memory_space=pltpu.SMEM),
                  pl.BlockSpec(memory_space=pl.ANY)],
        out_specs=pl.BlockSpec((PERM_CHUNK, w), lambda i: (i, 0)),
        out_shape=jax.ShapeDtypeStruct((n, w), src.dtype),
        scratch_shapes=[pltpu.SemaphoreType.DMA(())],
        compiler_params=pltpu.CompilerParams(dimension_semantics=("arbitrary",), has_side_effects=True),
        name="gather_rows",
    )(idx, src)


MOE_TILE = 256


def _moe_kernel(src_ref, ea_ref, eb_ref, nused_ref, xs_ref, gua_ref, gub_ref, da_ref, db_ref, y_ref):
    del src_ref, ea_ref, eb_ref

    @pl.when(pl.program_id(0) >= nused_ref[0])
    def _():
        y_ref[...] = jnp.zeros_like(y_ref)

    @pl.when(pl.program_id(0) < nused_ref[0])
    def _():
        x = xs_ref[:, :D_MODEL].astype(BF16)
        tail = xs_ref[:, D_MODEL:]
        lane = lax.broadcasted_iota(jnp.int32, tail.shape, 1)
        w_lo = jnp.sum(jnp.where(lane == 0, tail, 0.0), axis=1, keepdims=True)
        w_hi = jnp.sum(jnp.where(lane == 1, tail, 0.0), axis=1, keepdims=True)

        def expert(gu_ref, d_ref):
            gu = _dot(x, gu_ref[...])
            g_in, u_in = gu[:, :D_EXPERT], gu[:, D_EXPERT:]
            act = (g_in * jax.nn.sigmoid(g_in)) * u_in
            return _dot(act.astype(BF16), d_ref[...])

        y_ref[...] = w_lo * expert(gua_ref, da_ref) + w_hi * expert(gub_ref, db_ref)


def _moe_ffn(xs, tile_src, tile_ea, tile_eb, n_used, w_gate_up, w_down):
    n_pad = xs.shape[0]
    n_tiles = n_pad // MOE_TILE
    grid_spec = pltpu.PrefetchScalarGridSpec(
        num_scalar_prefetch=4,
        grid=(n_tiles,),
        in_specs=[pl.BlockSpec((MOE_TILE, D_MODEL + TAIL), lambda t, src, ea, eb, nu: (src[t], 0)),
                  pl.BlockSpec((None, D_MODEL, 2 * D_EXPERT), lambda t, src, ea, eb, nu: (ea[t], 0, 0)),
                  pl.BlockSpec((None, D_MODEL, 2 * D_EXPERT), lambda t, src, ea, eb, nu: (eb[t], 0, 0)),
                  pl.BlockSpec((None, D_EXPERT, D_MODEL), lambda t, src, ea, eb, nu: (ea[t], 0, 0)),
                  pl.BlockSpec((None, D_EXPERT, D_MODEL), lambda t, src, ea, eb, nu: (eb[t], 0, 0))],
        out_specs=pl.BlockSpec((MOE_TILE, D_MODEL), lambda t, src, ea, eb, nu: (t, 0)),
    )
    return pl.pallas_call(
        _moe_kernel,
        grid_spec=grid_spec,
        out_shape=jax.ShapeDtypeStruct((n_pad, D_MODEL), F32),
        compiler_params=_cparams("arbitrary"),
        name="moe_ffn",
    )(tile_src, tile_ea, tile_eb, n_used, xs, w_gate_up, w_gate_up, w_down, w_down)


def _routing_plan(rec, counts, n):
    n_tiles = n // MOE_TILE + N_BUCKETS
    cnt = counts[:N_BUCKETS, 0].astype(jnp.int32)
    tiles_per = (cnt + MOE_TILE - 1) // MOE_TILE
    tile_end = jnp.cumsum(tiles_per)
    tile_start = tile_end - tiles_per
    bucket = rec[2].astype(jnp.int32)
    rank = rec[3].astype(jnp.int32)
    dest = tile_start[bucket] * MOE_TILE + rank
    n_used = tile_end[-1]
    tile_id = jnp.clip(jnp.arange(n_tiles, dtype=jnp.int32), 0, jnp.maximum(n_used - 1, 0))
    tile_bucket = jnp.sum(tile_id[:, None] >= tile_end[None, :], axis=1).astype(jnp.int32)
    group = tile_bucket // 6
    pair = tile_bucket % 6
    lo = jnp.where(pair < 3, 0, jnp.where(pair < 5, 1, 2))
    hi = jnp.where(pair < 3, pair + 1, jnp.where(pair < 5, pair - 1, 3))
    return dest, tile_id, group * 4 + lo, group * 4 + hi, n_used.reshape(1).astype(jnp.int32), n_tiles * MOE_TILE


def _final_kernel(hx_ref, moe_ref, p_ref, wg_ref, wp_ref, g_ref, b_ref, o_ref):
    h = hx_ref[...]
    gate = jax.nn.sigmoid(_dot(h.astype(BF16), wg_ref[...]))
    ple = gate * _dot(p_ref[...].astype(BF16), wp_ref[...])
    o_ref[...] = _layer_norm(DEEPNORM_ALPHA * h + moe_ref[...] + ple, g_ref[...], b_ref[...])


def _final(hx, moe, p, w_gate, w_proj, ln_g, ln_b):
    n = moe.shape[0]
    tm = 512
    row = lambda i: (0, 0)
    return pl.pallas_call(
        _final_kernel,
        grid=(n // tm,),
        in_specs=[pl.BlockSpec((tm, D_MODEL), lambda i: (i, 0)),
                  pl.BlockSpec((tm, D_MODEL), lambda i: (i, 0)),
                  pl.BlockSpec((tm, PLE_DIM), lambda i: (i, 0)),
                  pl.BlockSpec((D_MODEL, D_MODEL), row),
                  pl.BlockSpec((PLE_DIM, D_MODEL), row),
                  pl.BlockSpec((1, D_MODEL), row),
                  pl.BlockSpec((1, D_MODEL), row)],
        out_specs=pl.BlockSpec((tm, D_MODEL), lambda i: (i, 0)),
        out_shape=jax.ShapeDtypeStruct((n, D_MODEL), F32),
        compiler_params=_cparams("parallel"),
        name="final",
    )(hx, moe, p, w_gate, w_proj, ln_g, ln_b)


def _pad_lanes(a, width=LANES):
    return jnp.pad(a, [(0, 0)] * (a.ndim - 1) + [(0, width - a.shape[-1])])


def _moe_and_norm(o, w_out, x, p_i, ln1_g, ln1_b, router, w_gate_up, w_down, ple_w_gate, ple_w_proj,
                  ln2_g, ln2_b):
    n = x.shape[0]
    hx, rec, counts = _post_mix(o, w_out.astype(BF16), x, ln1_g[None], ln1_b[None], *router)
    dest, tile_src, tile_ea, tile_eb, n_used, n_pad = _routing_plan(rec, counts, n)
    xs = _scatter_rows(hx, dest, n_pad)
    ys = _moe_ffn(xs, tile_src, tile_ea, tile_eb, n_used, w_gate_up.astype(BF16), w_down.astype(BF16))
    moe = _gather_rows(ys, dest)
    return _final(hx, moe, p_i, ple_w_gate.astype(BF16), ple_w_proj.astype(BF16), ln2_g[None], ln2_b[None])


def kernel(x, p, sb_w_in, sb_w_out, gdn_w_in, gdn_conv_w, gdn_a_log, gdn_dt_bias, gdn_norm_w, gdn_w_out,
           ln1_g, ln1_b, router_w, router_b, expert_w_gate_up, expert_w_down, ple_w_gate, ple_w_proj,
           ln2_g, ln2_b):
    bsz, t, d = x.shape
    n = bsz * t
    x = x.reshape(n, d)
    p = p.reshape(DEPTH, n, PLE_DIM)
    rw_pad = _pad_lanes(router_w)
    rw_hi = rw_pad.astype(BF16)
    router = (rw_hi, (rw_pad - rw_hi.astype(F32)).astype(BF16), router_b[:, None])

    qkv = _linear(x, sb_w_in[0].astype(BF16), BF16, 1024, 1024)
    o = _sb_attention(qkv.reshape(bsz, t, 3 * d), bsz, t).reshape(n, d)
    x = _moe_and_norm(o, sb_w_out[0], x, p[0], ln1_g[0], ln1_b[0], router, expert_w_gate_up[0],
                      expert_w_down[0], ple_w_gate[0], ple_w_proj[0], ln2_g[0], ln2_b[0])

    w_in = gdn_w_in[0]
    split = GDN_QKV_DIM + GDN_VAL_DIM
    proj = _linear(x, w_in[:, :split].astype(BF16), BF16, 1024, 1024)
    ba = _linear(x, _pad_lanes(w_in[:, split:]).astype(BF16), F32, 1024, LANES)
    proj = proj.reshape(bsz, t, split)
    qkv_c, kt = _gdn_conv(proj, gdn_conv_w[0], bsz, t)
    head_pad = lambda a: jnp.pad(a, (GDN_V_HEADS, LANES - 2 * GDN_V_HEADS))[None]
    gb, gbt = _gdn_gate_prep(ba.reshape(bsz, t, LANES), head_pad(gdn_a_log[0]), head_pad(gdn_dt_bias[0]), bsz, t)
    o = _gdn_recurrence(qkv_c, kt, proj, gb, gbt, gdn_norm_w[0][None], bsz, t).reshape(n, GDN_VAL_DIM)
    x = _moe_and_norm(o, gdn_w_out[0], x, p[1], ln1_g[1], ln1_b[1], router, expert_w_gate_up[1],
                      expert_w_down[1], ple_w_gate[1], ple_w_proj[1], ln2_g[1], ln2_b[1])
    return x.reshape(bsz, t, d)
```

```python
import functools

import jax
import jax.numpy as jnp
from jax import lax
from jax.experimental import pallas as pl
from jax.experimental.pallas import tpu as pltpu

D_MODEL = 1024
DEPTH = 2
PLE_DIM = 256
SB_HEADS = 16
SB_HEAD_DIM = 64
GDN_QK_HEADS = 8
GDN_V_HEADS = 16
GDN_HEAD_DIM = 128
GDN_KEY_DIM = GDN_QK_HEADS * GDN_HEAD_DIM
GDN_VAL_DIM = GDN_V_HEADS * GDN_HEAD_DIM
GDN_QKV_DIM = 2 * GDN_KEY_DIM + GDN_VAL_DIM
CONV_K = 4
CHUNK = 64
N_EXPERTS = 16
N_GROUPS = 4
EXPERTS_PER_GROUP = 4
D_EXPERT = 512
N_BUCKETS = 24
LN_EPS = 1e-5
RMS_EPS = 1e-6
DEEPNORM_ALPHA = (2 * DEPTH) ** 0.25

LANES = 128
TAIL = LANES
VMEM_LIMIT = 48 * 1024 * 1024
SB_DEAD_LOG2 = -151.0
LOG2_E = 1.4426950408889634

F32 = jnp.float32
BF16 = jnp.bfloat16


def _cparams(*sem):
    return pltpu.CompilerParams(dimension_semantics=sem, vmem_limit_bytes=VMEM_LIMIT)


def _dot(a, b):
    return jnp.dot(a, b, preferred_element_type=F32)


def _dot_nt(a, b):
    return lax.dot_general(a, b, (((1,), (1,)), ((), ())), preferred_element_type=F32)


def _split_dot(a, b_bf16, parts):
    acc = None
    rem = a
    for _ in range(parts):
        hi = rem.astype(BF16)
        term = _dot(hi, b_bf16)
        acc = term if acc is None else acc + term
        rem = rem - hi.astype(F32)
    return acc


def _linear_kernel(x_ref, w_ref, o_ref):
    o_ref[...] = _dot(x_ref[...].astype(BF16), w_ref[...]).astype(o_ref.dtype)


def _linear(x, w, out_dtype, tm, tn):
    m, k = x.shape
    n = w.shape[1]
    return pl.pallas_call(
        _linear_kernel,
        grid=(m // tm, n // tn),
        in_specs=[pl.BlockSpec((tm, k), lambda i, j: (i, 0)),
                  pl.BlockSpec((k, tn), lambda i, j: (0, j))],
        out_specs=pl.BlockSpec((tm, tn), lambda i, j: (i, j)),
        out_shape=jax.ShapeDtypeStruct((m, n), out_dtype),
        compiler_params=_cparams("parallel", "parallel"),
        name="linear",
    )(x, w)


SB_TQ = 128
SB_TK = 128
SB_GROUP = 8
SB_PIECE = 64


def _sb_kernel(q_ref, k_ref, v_ref, o_ref, carry_ref, acc_ref, *, scale):
    t = q_ref.shape[0]
    tq, tk, grp = SB_TQ, SB_TK, SB_GROUP
    rows = 2 * tq
    nrow = grp * rows
    npiece = nrow // SB_PIECE
    lane = lax.broadcasted_iota(jnp.int32, (tq, LANES), 1)
    first_head = lane < SB_HEAD_DIM
    jj = lax.broadcasted_iota(jnp.int32, (2 * tk, 2 * tk), 0) % tk
    ss = lax.broadcasted_iota(jnp.int32, (2 * tk, 2 * tk), 1)
    cum_mat = jnp.where((jj > ss) | (ss >= tk), 1.0, 0.0).astype(BF16)
    rp = lax.broadcasted_iota(jnp.int32, (SB_PIECE, tk), 0)
    cp = lax.broadcasted_iota(jnp.int32, (SB_PIECE, tk), 1)
    pieces_per_half = tq // SB_PIECE
    diag_mask = [cp < rp + (p % pieces_per_half) * SB_PIECE for p in range(pieces_per_half)]
    piece = lambda a, p: a[p * SB_PIECE:(p + 1) * SB_PIECE]

    def group(gi, _):
        first_tile = gi * grp
        qs = []
        for a in range(grp):
            q = q_ref[pl.ds(pl.multiple_of((first_tile + a) * tq, tq), tq), :] * scale
            zero = jnp.zeros_like(q)
            qs += [jnp.where(first_head, q, zero), jnp.where(first_head, zero, q)]
        qq = jnp.concatenate(qs, axis=0).reshape(grp, rows, LANES)
        carry_ref[...] = jnp.zeros_like(carry_ref)
        acc_ref[...] = jnp.zeros_like(acc_ref)

        def alive(c):
            d, top = c
            return jnp.logical_and(d < first_tile + grp, top >= SB_DEAD_LOG2)

        def step(d, diagonal):
            ks, vs = [], []
            for a in range(grp):
                k0 = pl.multiple_of(jnp.maximum(first_tile + a - d, 0) * tk, tk)
                ks.append(k_ref[pl.ds(k0, tk), :])
                vs.append(v_ref[pl.ds(k0, tk), :])
            z = jnp.einsum("gqd,gkd->gqk", qq, jnp.stack(ks), preferred_element_type=F32).reshape(nrow, tk)
            hls, lps = [], []
            for p in range(npiece):
                z2 = piece(z, p) * LOG2_E
                lb = jnp.minimum(z2, 0.0) - jnp.log2(1.0 + jnp.exp2(-jnp.abs(z2)))
                lm = lb - z2
                if diagonal:
                    mp = diag_mask[p % pieces_per_half]
                else:
                    mp = first_tile + p // (2 * pieces_per_half) - d >= 0
                lm = jnp.where(mp, lm, 0.0)
                hi = lm.astype(BF16)
                hls.append(jnp.concatenate([hi, (lm - hi.astype(F32)).astype(BF16)], axis=1))
                lps.append(jnp.where(mp, lb, -1e30))
            cs = _dot(jnp.concatenate(hls, axis=0), cum_mat)
            attn, top = [], None
            for p in range(npiece):
                rr = pl.ds(p * SB_PIECE, SB_PIECE)
                csp = piece(cs, p)
                carry = carry_ref[rr, :]
                attn.append(jnp.exp2(lps[p] + csp[:, :tk] + carry).astype(BF16))
                carry = carry + csp[:, tk:]
                if not diagonal:
                    carry = jnp.where(first_tile + p // (2 * pieces_per_half) - d >= 0, carry, -1e30)
                carry_ref[rr, :] = carry
                top = carry if top is None else jnp.maximum(top, carry)
            av = jnp.einsum("gqk,gkd->gqd", jnp.concatenate(attn, axis=0).reshape(grp, rows, tk), jnp.stack(vs),
                            preferred_element_type=F32)
            acc_ref[...] += av.reshape(nrow, LANES)
            return d + 1, jnp.max(top)

        lax.while_loop(alive, lambda c: step(c[0], False), step(jnp.int32(0), True))
        for a in range(grp):
            acc = acc_ref[a * rows:(a + 1) * rows, :]
            o_ref[pl.ds(pl.multiple_of((first_tile + a) * tq, tq), tq), :] = jnp.where(
                first_head, acc[:tq], acc[tq:]).astype(o_ref.dtype)
        return 0

    lax.fori_loop(0, t // (tq * grp), group, 0)


def _sb_attention(qkv, bsz, t):
    ncol = D_MODEL // LANES
    nrow = SB_GROUP * 2 * SB_TQ
    kern = functools.partial(_sb_kernel, scale=SB_HEAD_DIM ** -0.5)
    return pl.pallas_call(
        kern,
        grid=(bsz, ncol),
        in_specs=[pl.BlockSpec((None, t, LANES), lambda b, h: (b, 0, h)),
                  pl.BlockSpec((None, t, LANES), lambda b, h: (b, 0, ncol + h)),
                  pl.BlockSpec((None, t, LANES), lambda b, h: (b, 0, 2 * ncol + h))],
        out_specs=pl.BlockSpec((None, t, LANES), lambda b, h: (b, 0, h)),
        out_shape=jax.ShapeDtypeStruct((bsz, t, D_MODEL), BF16),
        scratch_shapes=[pltpu.VMEM((nrow, SB_TK), F32), pltpu.VMEM((nrow, LANES), F32)],
        compiler_params=_cparams("parallel", "parallel"),
        name="sb_attention",
    )(qkv, qkv, qkv)


CONV_PIECE = 128


def _conv_kernel(x_ref, halo_ref, w_ref, o_ref, kt_ref, *, tt, tc):
    i = pl.program_id(1)
    j = pl.program_id(2)
    w = w_ref[...]
    q_blocks = GDN_KEY_DIM // tc

    def run(normalise, transpose):
        for pc in range(tt // CONV_PIECE):
            rows = slice(pc * CONV_PIECE, (pc + 1) * CONV_PIECE)
            if pc == 0:
                prev = jnp.where(i > 0, halo_ref[...].astype(F32), 0.0)
            else:
                prev = x_ref[pc * CONV_PIECE - 8:pc * CONV_PIECE, :].astype(F32)
            cur = x_ref[rows, :].astype(F32)
            ext = jnp.concatenate([prev, cur], axis=0)
            acc = cur * w[CONV_K - 1:CONV_K, :]
            for s in range(1, CONV_K):
                acc = acc + pltpu.roll(ext, s, 0)[8:] * w[CONV_K - 1 - s:CONV_K - s, :]
            y = acc * jax.nn.sigmoid(acc)
            if normalise:
                parts = []
                for h in range(tc // GDN_HEAD_DIM):
                    yh = y[:, h * GDN_HEAD_DIM:(h + 1) * GDN_HEAD_DIM]
                    parts.append(yh * lax.rsqrt(jnp.sum(yh * yh, axis=1, keepdims=True) + RMS_EPS))
                y = jnp.concatenate(parts, axis=1)
            o_ref[rows, :] = y.astype(o_ref.dtype)
            if transpose:
                kt_ref[:, rows] = y.T.astype(kt_ref.dtype)

    @pl.when(j < q_blocks)
    def _():
        run(True, False)

    @pl.when((j >= q_blocks) & (j < 2 * q_blocks))
    def _():
        run(True, True)

    @pl.when(j >= 2 * q_blocks)
    def _():
        run(False, False)


def _gdn_conv(proj, conv_w, bsz, t):
    tt, tc = 2048, 256
    kern = functools.partial(_conv_kernel, tt=tt, tc=tc)
    q_blocks = GDN_KEY_DIM // tc
    return pl.pallas_call(
        kern,
        grid=(bsz, t // tt, GDN_QKV_DIM // tc),
        in_specs=[pl.BlockSpec((None, tt, tc), lambda b, i, j: (b, i, j)),
                  pl.BlockSpec((None, 8, tc), lambda b, i, j: (b, jnp.maximum(i * (tt // 8) - 1, 0), j)),
                  pl.BlockSpec((CONV_K, tc), lambda b, i, j: (0, j))],
        out_specs=[pl.BlockSpec((None, tt, tc), lambda b, i, j: (b, i, j)),
                   pl.BlockSpec((None, tc, tt), lambda b, i, j: (b, jnp.clip(j - q_blocks, 0, q_blocks - 1), i))],
        out_shape=[jax.ShapeDtypeStruct((bsz, t, GDN_QKV_DIM), BF16),
                   jax.ShapeDtypeStruct((bsz, GDN_KEY_DIM, t), BF16)],
        compiler_params=_cparams("parallel", "parallel", "arbitrary"),
        name="gdn_conv",
    )(proj, proj, conv_w)


def _gate_prep_kernel(ba_ref, alog_ref, dtb_ref, gb_ref, gbt_ref, *, tt):
    ba = ba_ref[...]
    lane = lax.broadcasted_iota(jnp.int32, (tt, LANES), 1)
    beta = jax.nn.sigmoid(ba)
    xa = ba + dtb_ref[...]
    softplus = jnp.maximum(xa, 0.0) + jnp.log1p(jnp.exp(-jnp.abs(xa)))
    g = -jnp.exp(alog_ref[...]) * softplus
    r = lax.broadcasted_iota(jnp.int32, (tt, tt), 0)
    c = lax.broadcasted_iota(jnp.int32, (tt, tt), 1)
    tri = jnp.where((c <= r) & (c // CHUNK == r // CHUNK), 1.0, 0.0).astype(BF16)
    is_g = (lane >= GDN_V_HEADS) & (lane < 2 * GDN_V_HEADS)
    g = jnp.where(is_g, g, 0.0)
    rem = g
    gc = jnp.zeros_like(g)
    for _ in range(3):
        hi = rem.astype(BF16)
        gc = gc + _dot(tri, hi)
        rem = rem - hi.astype(F32)
    out = jnp.where(lane < GDN_V_HEADS, beta, gc)
    gb_ref[...] = out
    gbt_ref[...] = out.T


def _gdn_gate_prep(ba, a_log_row, dt_bias_row, bsz, t):
    tt = 512
    kern = functools.partial(_gate_prep_kernel, tt=tt)
    return pl.pallas_call(
        kern,
        grid=(bsz, t // tt),
        in_specs=[pl.BlockSpec((None, tt, LANES), lambda b, i: (b, i, 0)),
                  pl.BlockSpec((1, LANES), lambda b, i: (0, 0)),
                  pl.BlockSpec((1, LANES), lambda b, i: (0, 0))],
        out_specs=[pl.BlockSpec((None, tt, LANES), lambda b, i: (b, i, 0)),
                   pl.BlockSpec((None, LANES, tt), lambda b, i: (b, 0, i))],
        out_shape=[jax.ShapeDtypeStruct((bsz, t, LANES), F32),
                   jax.ShapeDtypeStruct((bsz, LANES, t), F32)],
        compiler_params=_cparams("parallel", "parallel"),
        name="gdn_gate_prep",
    )(ba, a_log_row, dt_bias_row)


GDN_TT = 512
GDN_HG = 16
GDN_PA = 4


def _bmm(a, b):
    return jnp.einsum("cik,ckj->cij", a, b, preferred_element_type=F32)


def _gdn_kernel(q_ref, k_ref, kt_ref, v_ref, z_ref, gb_ref, gbt_ref, nw_ref, o_ref,
                s_ref, u_ref, w_ref, at_ref, qg_ref, ktd_ref, el_ref):
    tt, hg_n = GDN_TT, GDN_HG
    nc = tt // CHUNK
    hg = pl.program_id(1)

    @pl.when(pl.program_id(2) == 0)
    def _():
        s_ref[...] = jnp.zeros_like(s_ref)

    lane = lax.broadcasted_iota(jnp.int32, (tt, LANES), 1)
    sub8 = lax.broadcasted_iota(jnp.int32, (hg_n, tt), 0)
    npair = nc // 2
    pr = 2 * CHUNK
    ri = lax.broadcasted_iota(jnp.int32, (1, CHUNK, pr), 1)
    ci = lax.broadcasted_iota(jnp.int32, (1, CHUNK, pr), 2)
    even = ci < CHUNK
    cj = ci % CHUNK
    lower_incl = ri >= cj
    strict = ri > cj
    eye = jnp.where(ri == cj, 1.0, 0.0)
    qscale = GDN_HEAD_DIM ** -0.5
    gb = gb_ref[...]
    rows8 = gbt_ref[pl.ds(pl.multiple_of(GDN_V_HEADS + hg * hg_n, 8), hg_n), :]
    pairs = lambda a: a.reshape(npair, pr, a.shape[-1])
    lane_pairs = lambda a: jnp.stack([a[:, m * pr:(m + 1) * pr] for m in range(npair)])
    side = lambda x: jnp.where(even, x[:, :CHUNK, :], x[:, CHUNK:, :])
    zero_bf = jnp.zeros((1, CHUNK, pr), BF16)
    block_diag = lambda x: jnp.concatenate([jnp.where(even, x, zero_bf), jnp.where(even, zero_bf, x)], axis=1)

    cat = lambda xs: jnp.concatenate(xs, axis=0)
    for s0 in range(0, hg_n, GDN_PA):
        heads = range(s0, s0 + GDN_PA)
        q3s, k3s, kt3s, kks, qks = {}, {}, {}, {}, {}
        for qh in sorted({s // 2 for s in heads}):
            cols = slice(qh * LANES, (qh + 1) * LANES)
            q3s[qh] = pairs(q_ref[:, cols].astype(F32) * qscale)
            k3s[qh] = pairs(k_ref[:, cols].astype(F32))
            kt3s[qh] = lane_pairs(kt_ref[cols, :])
            kks[qh] = side(_bmm(k3s[qh].astype(BF16), kt3s[qh]))
            qks[qh] = side(_bmm(q3s[qh].astype(BF16), kt3s[qh]))
        per_head = lambda d: cat([d[s // 2] for s in heads])
        q3, k3, kt3, kk, qk = per_head(q3s), per_head(k3s), per_head(kt3s), per_head(kks), per_head(qks)
        bcol = cat([pairs(jnp.sum(jnp.where(lane == hg * hg_n + s, gb, 0.0), axis=1, keepdims=True)) for s in heads])
        gcol = cat([pairs(jnp.sum(jnp.where(lane == GDN_V_HEADS + hg * hg_n + s, gb, 0.0), axis=1, keepdims=True))
                    for s in heads])
        grow = cat([lane_pairs(jnp.sum(jnp.where(sub8 == s, rows8, 0.0), axis=0, keepdims=True))
                    for s in heads])
        v3 = cat([pairs(v_ref[:, s * LANES:(s + 1) * LANES].astype(F32)) for s in heads])
        gcol = jnp.broadcast_to(gcol, (gcol.shape[0], pr, pr))
        bcol = jnp.broadcast_to(bcol, (bcol.shape[0], pr, pr))
        decay = jnp.where(lower_incl, jnp.exp(jnp.minimum(side(gcol) - grow, 0.0)), 0.0)
        a = jnp.where(strict, side(bcol) * kk * decay, 0.0)
        eg = jnp.exp(gcol)
        p = a.astype(BF16)
        t_inv = eye - a
        for _ in range(5):
            p = _bmm(p, block_diag(p)).astype(BF16)
            t_inv = t_inv + _bmm(t_inv.astype(BF16), block_diag(p))
        rhs = _bmm(block_diag(t_inv.astype(BF16)),
                   jnp.concatenate([v3 * bcol, k3 * (bcol * eg)], axis=2).astype(BF16))
        g_last = jnp.where(even[:, :1, :], gcol[:, CHUNK - 1:CHUNK, :], gcol[:, pr - 1:pr, :])
        hs = slice(s0, s0 + GDN_PA)
        u_ref[hs] = rhs[:, :, :LANES].reshape(GDN_PA, tt, LANES)
        w_ref[hs] = rhs[:, :, LANES:].astype(BF16).reshape(GDN_PA, tt, LANES)
        attn = (qk * decay).astype(BF16)
        at_ref[hs] = jnp.concatenate([attn[:, :, :CHUNK], attn[:, :, CHUNK:]], axis=1).reshape(GDN_PA, tt, CHUNK)
        qg_ref[hs] = (q3 * eg).astype(BF16).reshape(GDN_PA, tt, LANES)
        ktd = (kt3.astype(F32) * jnp.exp(g_last - grow)).astype(BF16)
        ktd_ref[hs] = jnp.stack([ktd[:, :, :CHUNK], ktd[:, :, CHUNK:]], axis=1).reshape(
            GDN_PA, nc, GDN_HEAD_DIM, CHUNK)
        e_last = jnp.exp(g_last)
        el_ref[hs] = jnp.stack([jnp.broadcast_to(e_last[:, :, :1], (e_last.shape[0], 1, LANES)),
                                jnp.broadcast_to(e_last[:, :, CHUNK:CHUNK + 1], (e_last.shape[0], 1, LANES))],
                               axis=1).reshape(GDN_PA, nc, 1, LANES)

    nw = nw_ref[...]

    def chunk(c, _):
        rr = pl.ds(pl.multiple_of(c * CHUNK, CHUNK), CHUNK)
        st = s_ref[...]
        st_bf = st.astype(BF16)
        v_new = u_ref[:, rr, :] - _bmm(w_ref[:, rr, :], st_bf)
        v_new_bf = v_new.astype(BF16)
        o_c = _bmm(qg_ref[:, rr, :], st_bf) + _bmm(at_ref[:, rr, :], v_new_bf)
        s_ref[...] = st * el_ref[:, c] + _bmm(ktd_ref[:, c], v_new_bf)
        o_n = o_c * lax.rsqrt(jnp.mean(o_c * o_c, axis=2, keepdims=True) + RMS_EPS) * nw
        zz = z_ref[rr, :].astype(F32)
        for s in range(hg_n):
            zs = zz[:, s * LANES:(s + 1) * LANES]
            o_ref[rr, s * LANES:(s + 1) * LANES] = (o_n[s] * (zs * jax.nn.sigmoid(zs))).astype(o_ref.dtype)
        return 0

    lax.fori_loop(0, nc, chunk, 0)


def _gdn_recurrence(qkv, kt, proj, gb, gbt, norm_w_row, bsz, t):
    tt, hg_n = GDN_TT, GDN_HG
    nc = tt // CHUNK
    qw = (hg_n // 2) * LANES
    vw = hg_n * LANES
    q_blk, v_blk, z_blk = GDN_KEY_DIM // qw, 2 * GDN_KEY_DIM // vw, GDN_QKV_DIM // vw
    return pl.pallas_call(
        _gdn_kernel,
        grid=(bsz, GDN_V_HEADS // hg_n, t // tt),
        in_specs=[pl.BlockSpec((None, tt, qw), lambda b, j, i: (b, i, j)),
                  pl.BlockSpec((None, tt, qw), lambda b, j, i: (b, i, q_blk + j)),
                  pl.BlockSpec((None, qw, tt), lambda b, j, i: (b, j, i)),
                  pl.BlockSpec((None, tt, vw), lambda b, j, i: (b, i, v_blk + j)),
                  pl.BlockSpec((None, tt, vw), lambda b, j, i: (b, i, z_blk + j)),
                  pl.BlockSpec((None, tt, LANES), lambda b, j, i: (b, i, 0)),
                  pl.BlockSpec((None, LANES, tt), lambda b, j, i: (b, 0, i)),
                  pl.BlockSpec((1, LANES), lambda b, j, i: (0, 0))],
        out_specs=pl.BlockSpec((None, tt, vw), lambda b, j, i: (b, i, j)),
        out_shape=jax.ShapeDtypeStruct((bsz, t, GDN_VAL_DIM), BF16),
        scratch_shapes=[pltpu.VMEM((hg_n, GDN_HEAD_DIM, GDN_HEAD_DIM), F32),
                        pltpu.VMEM((hg_n, tt, LANES), F32),
                        pltpu.VMEM((hg_n, tt, LANES), BF16),
                        pltpu.VMEM((hg_n, tt, CHUNK), BF16),
                        pltpu.VMEM((hg_n, tt, LANES), BF16),
                        pltpu.VMEM((hg_n, nc, GDN_HEAD_DIM, CHUNK), BF16),
                        pltpu.VMEM((hg_n, nc, 1, LANES), F32)],
        compiler_params=_cparams("parallel", "parallel", "arbitrary"),
        name="gdn_recurrence",
    )(qkv, qkv, kt, qkv, proj, gb, gbt, norm_w_row)


def _layer_norm(r, g, b):
    mu = jnp.mean(r, axis=1, keepdims=True)
    d = r - mu
    var = jnp.mean(d * d, axis=1, keepdims=True)
    return d * lax.rsqrt(var + LN_EPS) * g + b


CNT_ROWS = 32


def _post_mix_kernel(o_ref, w_ref, x_ref, g_ref, b_ref, rwh_ref, rwl_ref, rb_ref, hx_ref, rec_ref, cnt_ref, *, tm):
    @pl.when(pl.program_id(0) == 0)
    def _():
        cnt_ref[...] = jnp.zeros_like(cnt_ref)

    mix = _dot(o_ref[...], w_ref[...])
    h = _layer_norm(DEEPNORM_ALPHA * x_ref[...] + mix, g_ref[...], b_ref[...])
    hx_ref[:, :D_MODEL] = h

    h_hi = h.astype(BF16)
    h_lo = (h - h_hi.astype(F32)).astype(BF16)
    logits = _dot(h_hi, rwh_ref[...]) + _dot(h_lo, rwh_ref[...]) + _dot(h_hi, rwl_ref[...])
    scores = jax.nn.sigmoid(logits.T[:N_EXPERTS])
    biased = scores + rb_ref[...]
    sel = [biased[e:e + 1] for e in range(N_EXPERTS)]
    sc = [scores[e:e + 1] for e in range(N_EXPERTS)]
    best, best_score = None, None
    for gidx in range(N_GROUPS):
        a, b, c, d = sel[4 * gidx:4 * gidx + 4]
        p, q, r, s = jnp.maximum(a, b), jnp.minimum(a, b), jnp.maximum(c, d), jnp.minimum(c, d)
        gs = jnp.maximum(p, r) + jnp.maximum(jnp.minimum(p, r), jnp.maximum(q, s))
        if gidx == 0:
            best, best_score = jnp.zeros_like(gs, dtype=jnp.int32), gs
        else:
            upd = gs > best_score
            best = jnp.where(upd, gidx, best)
            best_score = jnp.where(upd, gs, best_score)
    pick = lambda rows: [jnp.where(best == 0, rows[j], jnp.where(best == 1, rows[4 + j], jnp.where(
        best == 2, rows[8 + j], rows[12 + j]))) for j in range(EXPERTS_PER_GROUP)]
    v, s4 = pick(sel), pick(sc)
    kept = []
    for j in range(EXPERTS_PER_GROUP):
        beaten = [jnp.where(v[i] >= v[j] if i < j else v[i] > v[j], 1, 0) for i in range(EXPERTS_PER_GROUP) if i != j]
        kept.append(beaten[0] + beaten[1] + beaten[2] < 2)
    zero = jnp.zeros_like(s4[0])
    denom = (jnp.where(kept[0], s4[0], zero) + jnp.where(kept[1], s4[1], zero)
             + jnp.where(kept[2], s4[2], zero) + jnp.where(kept[3], s4[3], zero))
    lo = jnp.where(kept[0], 0, jnp.where(kept[1], 1, 2))
    hi = jnp.where(kept[3], 3, jnp.where(kept[2], 2, 1))
    w_lo = jnp.where(kept[0], s4[0], jnp.where(kept[1], s4[1], s4[2])) / denom
    w_hi = jnp.where(kept[3], s4[3], jnp.where(kept[2], s4[2], s4[1])) / denom
    pair = jnp.where(lo == 0, hi - 1, jnp.where(lo == 1, hi + 1, 5))
    bucket = best * 6 + pair

    brow = lax.broadcasted_iota(jnp.int32, (CNT_ROWS, tm), 0)
    mine = brow == bucket
    onehot = jnp.where(mine, 1.0, 0.0)
    r_i = lax.broadcasted_iota(jnp.int32, (tm, tm), 0)
    c_i = lax.broadcasted_iota(jnp.int32, (tm, tm), 1)
    before = jnp.where(r_i < c_i, 1.0, 0.0).astype(BF16)
    cnt = cnt_ref[...]
    prefix = _dot(onehot.astype(BF16), before) + jnp.concatenate([cnt] * (tm // LANES), axis=1)
    rank = jnp.sum(jnp.where(mine, prefix, 0.0), axis=0, keepdims=True)
    cnt_ref[...] = cnt + jnp.sum(onehot, axis=1, keepdims=True)

    row8 = lax.broadcasted_iota(jnp.int32, (8, tm), 0)
    rec = jnp.where(row8 == 0, w_lo, jnp.where(row8 == 1, w_hi, jnp.where(
        row8 == 2, bucket.astype(F32), jnp.where(row8 == 3, rank, 0.0))))
    rec_ref[...] = rec
    hx_ref[:, D_MODEL:] = jnp.concatenate([rec, jnp.zeros((TAIL - 8, tm), F32)], axis=0).T


def _post_mix(o, w_out, x, ln_g, ln_b, rw_hi, rw_lo, rb_col):
    n, ko = o.shape
    tm = 512
    kern = functools.partial(_post_mix_kernel, tm=tm)
    row = lambda i: (0, 0)
    return pl.pallas_call(
        kern,
        grid=(n // tm,),
        in_specs=[pl.BlockSpec((tm, ko), lambda i: (i, 0)),
                  pl.BlockSpec((ko, D_MODEL), row),
                  pl.BlockSpec((tm, D_MODEL), lambda i: (i, 0)),
                  pl.BlockSpec((1, D_MODEL), row),
                  pl.BlockSpec((1, D_MODEL), row),
                  pl.BlockSpec((D_MODEL, LANES), row),
                  pl.BlockSpec((D_MODEL, LANES), row),
                  pl.BlockSpec((N_EXPERTS, 1), row)],
        out_specs=[pl.BlockSpec((tm, D_MODEL + TAIL), lambda i: (i, 0)),
                   pl.BlockSpec((8, tm), lambda i: (0, i)),
                   pl.BlockSpec((CNT_ROWS, LANES), row)],
        out_shape=[jax.ShapeDtypeStruct((n, D_MODEL + TAIL), F32),
                   jax.ShapeDtypeStruct((8, n), F32),
                   jax.ShapeDtypeStruct((CNT_ROWS, LANES), F32)],
        compiler_params=_cparams("arbitrary"),
        name="post_mix",
    )(o, w_out, x, ln_g, ln_b, rw_hi, rw_lo, rb_col)


PERM_CHUNK = 512
PERM_UNROLL = 8


def _row_copy(src_ref, dst_ref, sem, s, d):
    return pltpu.make_async_copy(src_ref.at[pl.ds(s, 1)], dst_ref.at[pl.ds(d, 1)], sem)


def _permute_rows_loops(copy_row):
    def issue(i, _):
        for u in range(PERM_UNROLL):
            copy_row(i * PERM_UNROLL + u).start()
        return 0

    def drain(i, _):
        for u in range(PERM_UNROLL):
            copy_row(0).wait()
        return 0

    lax.fori_loop(0, PERM_CHUNK // PERM_UNROLL, issue, 0)
    lax.fori_loop(0, PERM_CHUNK // PERM_UNROLL, drain, 0)


def _scatter_rows_kernel(idx_ref, src_ref, init_ref, out_ref, sem):
    del init_ref
    _permute_rows_loops(lambda r: _row_copy(src_ref, out_ref, sem, r, idx_ref[r]))


def _scatter_rows(src, dest, n_out):
    n, w = src.shape
    return pl.pallas_call(
        _scatter_rows_kernel,
        grid=(n // PERM_CHUNK,),
        in_specs=[pl.BlockSpec((PERM_CHUNK,), lambda i: (i,), memory_space=pltpu.SMEM),
                  pl.BlockSpec((PERM_CHUNK, w), lambda i: (i, 0)),
                  pl.BlockSpec(memory_space=pl.ANY)],
        out_specs=pl.BlockSpec(memory_space=pl.ANY),
        out_shape=jax.ShapeDtypeStruct((n_out, w), src.dtype),
        scratch_shapes=[pltpu.SemaphoreType.DMA(())],
        input_output_aliases={2: 0},
        compiler_params=pltpu.CompilerParams(dimension_semantics=("arbitrary",), has_side_effects=True),
        name="scatter_rows",
    )(dest, src, jnp.zeros((n_out, w), src.dtype))


def _gather_rows_kernel(idx_ref, src_ref, out_ref, sem):
    _permute_rows_loops(lambda r: _row_copy(src_ref, out_ref, sem, idx_ref[r], r))


def _gather_rows(src, idx):
    n = idx.shape[0]
    w = src.shape[1]
    return pl.pallas_call(
        _gather_rows_kernel,
        grid=(n // PERM_CHUNK,),
        in_specs=[pl.BlockSpec((PERM_CHUNK,), lambda i: (i,), memory_space=pltpu.SMEM),
                  pl.BlockSpec(memory_space=pl.ANY)],
        out_specs=pl.BlockSpec((PERM_CHUNK, w), lambda i: (i, 0)),
        out_shape=jax.ShapeDtypeStruct((n, w), src.dtype),
        scratch_shapes=[pltpu.SemaphoreType.DMA(())],
        compiler_params=pltpu.CompilerParams(dimension_semantics=("arbitrary",), has_side_effects=True),
        name="gather_rows",
    )(idx, src)


MOE_TILE = 256


def _moe_kernel(src_ref, ea_ref, eb_ref, nused_ref, xs_ref, gua_ref, gub_ref, da_ref, db_ref, y_ref):
    del src_ref, ea_ref, eb_ref

    @pl.when(pl.program_id(0) >= nused_ref[0])
    def _():
        y_ref[...] = jnp.zeros_like(y_ref)

    @pl.when(pl.program_id(0) < nused_ref[0])
    def _():
        x = xs_ref[:, :D_MODEL].astype(BF16)
        tail = xs_ref[:, D_MODEL:]
        lane = lax.broadcasted_iota(jnp.int32, tail.shape, 1)
        w_lo = jnp.sum(jnp.where(lane == 0, tail, 0.0), axis=1, keepdims=True)
        w_hi = jnp.sum(jnp.where(lane == 1, tail, 0.0), axis=1, keepdims=True)

        def expert(gu_ref, d_ref):
            gu = _dot(x, gu_ref[...])
            g_in, u_in = gu[:, :D_EXPERT], gu[:, D_EXPERT:]
            act = (g_in * jax.nn.sigmoid(g_in)) * u_in
            return _dot(act.astype(BF16), d_ref[...])

        y_ref[...] = w_lo * expert(gua_ref, da_ref) + w_hi * expert(gub_ref, db_ref)


def _moe_ffn(xs, tile_src, tile_ea, tile_eb, n_used, w_gate_up, w_down):
    n_pad = xs.shape[0]
    n_tiles = n_pad // MOE_TILE
    grid_spec = pltpu.PrefetchScalarGridSpec(
        num_scalar_prefetch=4,
        grid=(n_tiles,),
        in_specs=[pl.BlockSpec((MOE_TILE, D_MODEL + TAIL), lambda t, src, ea, eb, nu: (src[t], 0)),
                  pl.BlockSpec((None, D_MODEL, 2 * D_EXPERT), lambda t, src, ea, eb, nu: (ea[t], 0, 0)),
                  pl.BlockSpec((None, D_MODEL, 2 * D_EXPERT), lambda t, src, ea, eb, nu: (eb[t], 0, 0)),
                  pl.BlockSpec((None, D_EXPERT, D_MODEL), lambda t, src, ea, eb, nu: (ea[t], 0, 0)),
                  pl.BlockSpec((None, D_EXPERT, D_MODEL), lambda t, src, ea, eb, nu: (eb[t], 0, 0))],
        out_specs=pl.BlockSpec((MOE_TILE, D_MODEL), lambda t, src, ea, eb, nu: (t, 0)),
    )
    return pl.pallas_call(
        _moe_kernel,
        grid_spec=grid_spec,
        out_shape=jax.ShapeDtypeStruct((n_pad, D_MODEL), F32),
        compiler_params=_cparams("arbitrary"),
        name="moe_ffn",
    )(tile_src, tile_ea, tile_eb, n_used, xs, w_gate_up, w_gate_up, w_down, w_down)


def _routing_plan(rec, counts, n):
    n_tiles = n // MOE_TILE + N_BUCKETS
    cnt = counts[:N_BUCKETS, 0].astype(jnp.int32)
    tiles_per = (cnt + MOE_TILE - 1) // MOE_TILE
    tile_end = jnp.cumsum(tiles_per)
    tile_start = tile_end - tiles_per
    bucket = rec[2].astype(jnp.int32)
    rank = rec[3].astype(jnp.int32)
    dest = tile_start[bucket] * MOE_TILE + rank
    n_used = tile_end[-1]
    tile_id = jnp.clip(jnp.arange(n_tiles, dtype=jnp.int32), 0, jnp.maximum(n_used - 1, 0))
    tile_bucket = jnp.sum(tile_id[:, None] >= tile_end[None, :], axis=1).astype(jnp.int32)
    group = tile_bucket // 6
    pair = tile_bucket % 6
    lo = jnp.where(pair < 3, 0, jnp.where(pair < 5, 1, 2))
    hi = jnp.where(pair < 3, pair + 1, jnp.where(pair < 5, pair - 1, 3))
    return dest, tile_id, group * 4 + lo, group * 4 + hi, n_used.reshape(1).astype(jnp.int32), n_tiles * MOE_TILE


def _final_kernel(hx_ref, moe_ref, p_ref, wg_ref, wp_ref, g_ref, b_ref, o_ref):
    h = hx_ref[...]
    gate = jax.nn.sigmoid(_dot(h.astype(BF16), wg_ref[...]))
    ple = gate * _dot(p_ref[...].astype(BF16), wp_ref[...])
    o_ref[...] = _layer_norm(DEEPNORM_ALPHA * h + moe_ref[...] + ple, g_ref[...], b_ref[...])


def _final(hx, moe, p, w_gate, w_proj, ln_g, ln_b):
    n = moe.shape[0]
    tm = 512
    row = lambda i: (0, 0)
    return pl.pallas_call(
        _final_kernel,
        grid=(n // tm,),
        in_specs=[pl.BlockSpec((tm, D_MODEL), lambda i: (i, 0)),
                  pl.BlockSpec((tm, D_MODEL), lambda i: (i, 0)),
                  pl.BlockSpec((tm, PLE_DIM), lambda i: (i, 0)),
                  pl.BlockSpec((D_MODEL, D_MODEL), row),
                  pl.BlockSpec((PLE_DIM, D_MODEL), row),
                  pl.BlockSpec((1, D_MODEL), row),
                  pl.BlockSpec((1, D_MODEL), row)],
        out_specs=pl.BlockSpec((tm, D_MODEL), lambda i: (i, 0)),
        out_shape=jax.ShapeDtypeStruct((n, D_MODEL), F32),
        compiler_params=_cparams("parallel"),
        name="final",
    )(hx, moe, p, w_gate, w_proj, ln_g, ln_b)


def _pad_lanes(a, width=LANES):
    return jnp.pad(a, [(0, 0)] * (a.ndim - 1) + [(0, width - a.shape[-1])])


def _moe_and_norm(o, w_out, x, p_i, ln1_g, ln1_b, router, w_gate_up, w_down, ple_w_gate, ple_w_proj,
                  ln2_g, ln2_b):
    n = x.shape[0]
    hx, rec, counts = _post_mix(o, w_out.astype(BF16), x, ln1_g[None], ln1_b[None], *router)
    dest, tile_src, tile_ea, tile_eb, n_used, n_pad = _routing_plan(rec, counts, n)
    xs = _scatter_rows(hx, dest, n_pad)
    ys = _moe_ffn(xs, tile_src, tile_ea, tile_eb, n_used, w_gate_up.astype(BF16), w_down.astype(BF16))
    moe = _gather_rows(ys, dest)
    return _final(hx, moe, p_i, ple_w_gate.astype(BF16), ple_w_proj.astype(BF16), ln2_g[None], ln2_b[None])


def kernel(x, p, sb_w_in, sb_w_out, gdn_w_in, gdn_conv_w, gdn_a_log, gdn_dt_bias, gdn_norm_w, gdn_w_out,
           ln1_g, ln1_b, router_w, router_b, expert_w_gate_up, expert_w_down, ple_w_gate, ple_w_proj,
           ln2_g, ln2_b):
    bsz, t, d = x.shape
    n = bsz * t
    x = x.reshape(n, d)
    p = p.reshape(DEPTH, n, PLE_DIM)
    rw_pad = _pad_lanes(router_w)
    rw_hi = rw_pad.astype(BF16)
    router = (rw_hi, (rw_pad - rw_hi.astype(F32)).astype(BF16), router_b[:, None])

    qkv = _linear(x, sb_w_in[0].astype(BF16), BF16, 1024, 1024)
    o = _sb_attention(qkv.reshape(bsz, t, 3 * d), bsz, t).reshape(n, d)
    x = _moe_and_norm(o, sb_w_out[0], x, p[0], ln1_g[0], ln1_b[0], router, expert_w_gate_up[0],
                      expert_w_down[0], ple_w_gate[0], ple_w_proj[0], ln2_g[0], ln2_b[0])

    w_in = gdn_w_in[0]
    split = GDN_QKV_DIM + GDN_VAL_DIM
    proj = _linear(x, w_in[:, :split].astype(BF16), BF16, 1024, 1024)
    ba = _linear(x, _pad_lanes(w_in[:, split:]).astype(BF16), F32, 1024, LANES)
    proj = proj.reshape(bsz, t, split)
    qkv_c, kt = _gdn_conv(proj, gdn_conv_w[0], bsz, t)
    head_pad = lambda a: jnp.pad(a, (GDN_V_HEADS, LANES - 2 * GDN_V_HEADS))[None]
    gb, gbt = _gdn_gate_prep(ba.reshape(bsz, t, LANES), head_pad(gdn_a_log[0]), head_pad(gdn_dt_bias[0]), bsz, t)
    o = _gdn_recurrence(qkv_c, kt, proj, gb, gbt, gdn_norm_w[0][None], bsz, t).reshape(n, GDN_VAL_DIM)
    x = _moe_and_norm(o, gdn_w_out[0], x, p[1], ln1_g[1], ln1_b[1], router, expert_w_gate_up[1],
                      expert_w_down[1], ple_w_gate[1], ple_w_proj[1], ln2_g[1], ln2_b[1])
    return x.reshape(bsz, t, d)
```

```python
import functools

import jax
import jax.numpy as jnp
from jax import lax
from jax.experimental import pallas as pl
from jax.experimental.pallas import tpu as pltpu

D_MODEL = 1024
DEPTH = 2
PLE_DIM = 256
SB_HEAD_DIM = 64
GDN_QK_HEADS = 8
GDN_V_HEADS = 16
GDN_HEAD_DIM = 128
GDN_KEY_DIM = GDN_QK_HEADS * GDN_HEAD_DIM
GDN_VAL_DIM = GDN_V_HEADS * GDN_HEAD_DIM
GDN_QKV_DIM = 2 * GDN_KEY_DIM + GDN_VAL_DIM
CONV_K = 4
CHUNK = 64
N_EXPERTS = 16
N_GROUPS = 4
EXPERTS_PER_GROUP = 4
D_EXPERT = 512
N_BUCKETS = 24
LN_EPS = 1e-5
RMS_EPS = 1e-6
DEEPNORM_ALPHA = (2 * DEPTH) ** 0.25

LANES = 128
TAIL = LANES
VMEM_LIMIT = 48 * 1024 * 1024
SB_DEAD_LOG2 = -151.0
LOG2_E = 1.4426950408889634

F32 = jnp.float32
BF16 = jnp.bfloat16


def _cparams(*sem):
    return pltpu.CompilerParams(dimension_semantics=sem, vmem_limit_bytes=VMEM_LIMIT)


def _dot(a, b):
    return jnp.dot(a, b, preferred_element_type=F32)


def _linear_kernel(x_ref, w_ref, o_ref):
    o_ref[...] = _dot(x_ref[...].astype(BF16), w_ref[...]).astype(o_ref.dtype)


def _linear(x, w, out_dtype, tm, tn):
    m, k = x.shape
    n = w.shape[1]
    return pl.pallas_call(
        _linear_kernel,
        grid=(m // tm, n // tn),
        in_specs=[pl.BlockSpec((tm, k), lambda i, j: (i, 0)),
                  pl.BlockSpec((k, tn), lambda i, j: (0, j))],
        out_specs=pl.BlockSpec((tm, tn), lambda i, j: (i, j)),
        out_shape=jax.ShapeDtypeStruct((m, n), out_dtype),
        compiler_params=_cparams("parallel", "parallel"),
        name="linear",
    )(x, w)


SB_TQ = 128
SB_TK = 128
SB_GROUP = 8
SB_PIECE = 64


def _sb_kernel(q_ref, k_ref, v_ref, o_ref, carry_ref, acc_ref, *, scale):
    t = q_ref.shape[0]
    tq, tk, grp = SB_TQ, SB_TK, SB_GROUP
    rows = 2 * tq
    nrow = grp * rows
    npiece = nrow // SB_PIECE
    lane = lax.broadcasted_iota(jnp.int32, (tq, LANES), 1)
    first_head = lane < SB_HEAD_DIM
    jj = lax.broadcasted_iota(jnp.int32, (2 * tk, 2 * tk), 0) % tk
    ss = lax.broadcasted_iota(jnp.int32, (2 * tk, 2 * tk), 1)
    cum_mat = jnp.where((jj > ss) | (ss >= tk), 1.0, 0.0).astype(BF16)
    rp = lax.broadcasted_iota(jnp.int32, (SB_PIECE, tk), 0)
    cp = lax.broadcasted_iota(jnp.int32, (SB_PIECE, tk), 1)
    pieces_per_half = tq // SB_PIECE
    diag_mask = [cp < rp + (p % pieces_per_half) * SB_PIECE for p in range(pieces_per_half)]
    piece = lambda a, p: a[p * SB_PIECE:(p + 1) * SB_PIECE]

    def group(gi, _):
        first_tile = gi * grp
        qs = []
        for a in range(grp):
            q = q_ref[pl.ds(pl.multiple_of((first_tile + a) * tq, tq), tq), :] * scale
            zero = jnp.zeros_like(q)
            qs += [jnp.where(first_head, q, zero), jnp.where(first_head, zero, q)]
        qq = jnp.concatenate(qs, axis=0).reshape(grp, rows, LANES)
        carry_ref[...] = jnp.zeros_like(carry_ref)
        acc_ref[...] = jnp.zeros_like(acc_ref)

        def alive(c):
            d, top = c
            return jnp.logical_and(d < first_tile + grp, top >= SB_DEAD_LOG2)

        def step(d, diagonal):
            ks, vs = [], []
            for a in range(grp):
                k0 = pl.multiple_of(jnp.maximum(first_tile + a - d, 0) * tk, tk)
                ks.append(k_ref[pl.ds(k0, tk), :])
                vs.append(v_ref[pl.ds(k0, tk), :])
            z = jnp.einsum("gqd,gkd->gqk", qq, jnp.stack(ks), preferred_element_type=F32).reshape(nrow, tk)
            hls, lps = [], []
            for p in range(npiece):
                z2 = piece(z, p) * LOG2_E
                lb = jnp.minimum(z2, 0.0) - jnp.log2(1.0 + jnp.exp2(-jnp.abs(z2)))
                lm = lb - z2
                if diagonal:
                    mp = diag_mask[p % pieces_per_half]
                else:
                    mp = first_tile + p // (2 * pieces_per_half) - d >= 0
                lm = jnp.where(mp, lm, 0.0)
                hi = lm.astype(BF16)
                hls.append(jnp.concatenate([hi, (lm - hi.astype(F32)).astype(BF16)], axis=1))
                lps.append(jnp.where(mp, lb, -1e30))
            cs = _dot(jnp.concatenate(hls, axis=0), cum_mat)
            attn, top = [], None
            for p in range(npiece):
                rr = pl.ds(p * SB_PIECE, SB_PIECE)
                csp = piece(cs, p)
                carry = carry_ref[rr, :]
                attn.append(jnp.exp2(lps[p] + csp[:, :tk] + carry).astype(BF16))
                carry = carry + csp[:, tk:]
                if not diagonal:
                    carry = jnp.where(first_tile + p // (2 * pieces_per_half) - d >= 0, carry, -1e30)
                carry_ref[rr, :] = carry
                top = carry if top is None else jnp.maximum(top, carry)
            av = jnp.einsum("gqk,gkd->gqd", jnp.concatenate(attn, axis=0).reshape(grp, rows, tk), jnp.stack(vs),
                            preferred_element_type=F32)
            acc_ref[...] += av.reshape(nrow, LANES)
            return d + 1, jnp.max(top)

        lax.while_loop(alive, lambda c: step(c[0], False), step(jnp.int32(0), True))
        for a in range(grp):
            acc = acc_ref[a * rows:(a + 1) * rows, :]
            o_ref[pl.ds(pl.multiple_of((first_tile + a) * tq, tq), tq), :] = jnp.where(
                first_head, acc[:tq], acc[tq:]).astype(o_ref.dtype)
        return 0

    lax.fori_loop(0, t // (tq * grp), group, 0)


def _sb_attention(qkv, bsz, t):
    ncol = D_MODEL // LANES
    nrow = SB_GROUP * 2 * SB_TQ
    kern = functools.partial(_sb_kernel, scale=SB_HEAD_DIM ** -0.5)
    return pl.pallas_call(
        kern,
        grid=(bsz, ncol),
        in_specs=[pl.BlockSpec((None, t, LANES), lambda b, h: (b, 0, h)),
                  pl.BlockSpec((None, t, LANES), lambda b, h: (b, 0, ncol + h)),
                  pl.BlockSpec((None, t, LANES), lambda b, h: (b, 0, 2 * ncol + h))],
        out_specs=pl.BlockSpec((None, t, LANES), lambda b, h: (b, 0, h)),
        out_shape=jax.ShapeDtypeStruct((bsz, t, D_MODEL), BF16),
        scratch_shapes=[pltpu.VMEM((nrow, SB_TK), F32), pltpu.VMEM((nrow, LANES), F32)],
        compiler_params=_cparams("parallel", "parallel"),
        name="sb_attention",
    )(qkv, qkv, qkv)


CONV_PIECE = 128


def _conv_kernel(x_ref, halo_ref, w_ref, o_ref, kt_ref, *, tt, tc):
    i = pl.program_id(1)
    j = pl.program_id(2)
    w = w_ref[...]
    q_blocks = GDN_KEY_DIM // tc

    def run(normalise, transpose):
        for pc in range(tt // CONV_PIECE):
            rows = slice(pc * CONV_PIECE, (pc + 1) * CONV_PIECE)
            if pc == 0:
                prev = jnp.where(i > 0, halo_ref[...].astype(F32), 0.0)
            else:
                prev = x_ref[pc * CONV_PIECE - 8:pc * CONV_PIECE, :].astype(F32)
            cur = x_ref[rows, :].astype(F32)
            ext = jnp.concatenate([prev, cur], axis=0)
            acc = cur * w[CONV_K - 1:CONV_K, :]
            for s in range(1, CONV_K):
                acc = acc + pltpu.roll(ext, s, 0)[8:] * w[CONV_K - 1 - s:CONV_K - s, :]
            y = acc * jax.nn.sigmoid(acc)
            if normalise:
                parts = []
                for h in range(tc // GDN_HEAD_DIM):
                    yh = y[:, h * GDN_HEAD_DIM:(h + 1) * GDN_HEAD_DIM]
                    parts.append(yh * lax.rsqrt(jnp.sum(yh * yh, axis=1, keepdims=True) + RMS_EPS))
                y = jnp.concatenate(parts, axis=1)
            o_ref[rows, :] = y.astype(o_ref.dtype)
            if transpose:
                kt_ref[:, rows] = y.T.astype(kt_ref.dtype)

    @pl.when(j < q_blocks)
    def _():
        run(True, False)

    @pl.when((j >= q_blocks) & (j < 2 * q_blocks))
    def _():
        run(True, True)

    @pl.when(j >= 2 * q_blocks)
    def _():
        run(False, False)


def _gdn_conv(proj, conv_w, bsz, t):
    tt, tc = 2048, 256
    kern = functools.partial(_conv_kernel, tt=tt, tc=tc)
    q_blocks = GDN_KEY_DIM // tc
    return pl.pallas_call(
        kern,
        grid=(bsz, t // tt, GDN_QKV_DIM // tc),
        in_specs=[pl.BlockSpec((None, tt, tc), lambda b, i, j: (b, i, j)),
                  pl.BlockSpec((None, 8, tc), lambda b, i, j: (b, jnp.maximum(i * (tt // 8) - 1, 0), j)),
                  pl.BlockSpec((CONV_K, tc), lambda b, i, j: (0, j))],
        out_specs=[pl.BlockSpec((None, tt, tc), lambda b, i, j: (b, i, j)),
                   pl.BlockSpec((None, tc, tt), lambda b, i, j: (b, jnp.clip(j - q_blocks, 0, q_blocks - 1), i))],
        out_shape=[jax.ShapeDtypeStruct((bsz, t, GDN_QKV_DIM), BF16),
                   jax.ShapeDtypeStruct((bsz, GDN_KEY_DIM, t), BF16)],
        compiler_params=_cparams("parallel", "parallel", "arbitrary"),
        name="gdn_conv",
    )(proj, proj, conv_w)


def _gate_prep_kernel(ba_ref, alog_ref, dtb_ref, gb_ref, gbt_ref, *, tt):
    ba = ba_ref[...]
    lane = lax.broadcasted_iota(jnp.int32, (tt, LANES), 1)
    beta = jax.nn.sigmoid(ba)
    xa = ba + dtb_ref[...]
    softplus = jnp.maximum(xa, 0.0) + jnp.log1p(jnp.exp(-jnp.abs(xa)))
    g = -jnp.exp(alog_ref[...]) * softplus
    r = lax.broadcasted_iota(jnp.int32, (tt, tt), 0)
    c = lax.broadcasted_iota(jnp.int32, (tt, tt), 1)
    tri = jnp.where((c <= r) & (c // CHUNK == r // CHUNK), 1.0, 0.0).astype(BF16)
    is_g = (lane >= GDN_V_HEADS) & (lane < 2 * GDN_V_HEADS)
    g = jnp.where(is_g, g, 0.0)
    rem = g
    gc = jnp.zeros_like(g)
    for _ in range(3):
        hi = rem.astype(BF16)
        gc = gc + _dot(tri, hi)
        rem = rem - hi.astype(F32)
    out = jnp.where(lane < GDN_V_HEADS, beta, gc)
    gb_ref[...] = out
    gbt_ref[...] = out.T


def _gdn_gate_prep(ba, a_log_row, dt_bias_row, bsz, t):
    tt = 512
    kern = functools.partial(_gate_prep_kernel, tt=tt)
    return pl.pallas_call(
        kern,
        grid=(bsz, t // tt),
        in_specs=[pl.BlockSpec((None, tt, LANES), lambda b, i: (b, i, 0)),
                  pl.BlockSpec((1, LANES), lambda b, i: (0, 0)),
                  pl.BlockSpec((1, LANES), lambda b, i: (0, 0))],
        out_specs=[pl.BlockSpec((None, tt, LANES), lambda b, i: (b, i, 0)),
                   pl.BlockSpec((None, LANES, tt), lambda b, i: (b, 0, i))],
        out_shape=[jax.ShapeDtypeStruct((bsz, t, LANES), F32),
                   jax.ShapeDtypeStruct((bsz, LANES, t), F32)],
        compiler_params=_cparams("parallel", "parallel"),
        name="gdn_gate_prep",
    )(ba, a_log_row, dt_bias_row)


GDN_TT = 512
GDN_HG = 16
GDN_PA = 4


def _bmm(a, b):
    return jnp.einsum("cik,ckj->cij", a, b, preferred_element_type=F32)


def _gdn_kernel(q_ref, k_ref, kt_ref, v_ref, z_ref, gb_ref, gbt_ref, nw_ref, o_ref,
                s_ref, u_ref, w_ref, at_ref, qg_ref, ktd_ref, el_ref):
    tt, hg_n = GDN_TT, GDN_HG
    nc = tt // CHUNK
    hg = pl.program_id(1)

    @pl.when(pl.program_id(2) == 0)
    def _():
        s_ref[...] = jnp.zeros_like(s_ref)

    lane = lax.broadcasted_iota(jnp.int32, (tt, LANES), 1)
    sub8 = lax.broadcasted_iota(jnp.int32, (hg_n, tt), 0)
    npair = nc // 2
    pr = 2 * CHUNK
    ri = lax.broadcasted_iota(jnp.int32, (1, CHUNK, pr), 1)
    ci = lax.broadcasted_iota(jnp.int32, (1, CHUNK, pr), 2)
    even = ci < CHUNK
    cj = ci % CHUNK
    lower_incl = ri >= cj
    strict = ri > cj
    eye = jnp.where(ri == cj, 1.0, 0.0)
    qscale = GDN_HEAD_DIM ** -0.5
    gb = gb_ref[...]
    rows8 = gbt_ref[pl.ds(pl.multiple_of(GDN_V_HEADS + hg * hg_n, 8), hg_n), :]
    pairs = lambda a: a.reshape(npair, pr, a.shape[-1])
    lane_pairs = lambda a: jnp.stack([a[:, m * pr:(m + 1) * pr] for m in range(npair)])
    side = lambda x: jnp.where(even, x[:, :CHUNK, :], x[:, CHUNK:, :])
    zero_bf = jnp.zeros((1, CHUNK, pr), BF16)
    block_diag = lambda x: jnp.concatenate([jnp.where(even, x, zero_bf), jnp.where(even, zero_bf, x)], axis=1)

    cat = lambda xs: jnp.concatenate(xs, axis=0)
    for s0 in range(0, hg_n, GDN_PA):
        heads = range(s0, s0 + GDN_PA)
        q3s, k3s, kt3s, kks, qks = {}, {}, {}, {}, {}
        for qh in sorted({s // 2 for s in heads}):
            cols = slice(qh * LANES, (qh + 1) * LANES)
            q3s[qh] = pairs(q_ref[:, cols].astype(F32) * qscale)
            k3s[qh] = pairs(k_ref[:, cols].astype(F32))
            kt3s[qh] = lane_pairs(kt_ref[cols, :])
            kks[qh] = side(_bmm(k3s[qh].astype(BF16), kt3s[qh]))
            qks[qh] = side(_bmm(q3s[qh].astype(BF16), kt3s[qh]))
        per_head = lambda d: cat([d[s // 2] for s in heads])
        q3, k3, kt3, kk, qk = per_head(q3s), per_head(k3s), per_head(kt3s), per_head(kks), per_head(qks)
        bcol = cat([pairs(jnp.sum(jnp.where(lane == hg * hg_n + s, gb, 0.0), axis=1, keepdims=True)) for s in heads])
        gcol = cat([pairs(jnp.sum(jnp.where(lane == GDN_V_HEADS + hg * hg_n + s, gb, 0.0), axis=1, keepdims=True))
                    for s in heads])
        grow = cat([lane_pairs(jnp.sum(jnp.where(sub8 == s, rows8, 0.0), axis=0, keepdims=True))
                    for s in heads])
        v3 = cat([pairs(v_ref[:, s * LANES:(s + 1) * LANES].astype(F32)) for s in heads])
        gcol = jnp.broadcast_to(gcol, (gcol.shape[0], pr, pr))
        bcol = jnp.broadcast_to(bcol, (bcol.shape[0], pr, pr))
        decay = jnp.where(lower_incl, jnp.exp(jnp.minimum(side(gcol) - grow, 0.0)), 0.0)
        a = jnp.where(strict, side(bcol) * kk * decay, 0.0)
        eg = jnp.exp(gcol)
        p = a.astype(BF16)
        t_inv = eye - a
        for _ in range(5):
            p = _bmm(p, block_diag(p)).astype(BF16)
            t_inv = t_inv + _bmm(t_inv.astype(BF16), block_diag(p))
        rhs = _bmm(block_diag(t_inv.astype(BF16)),
                   jnp.concatenate([v3 * bcol, k3 * (bcol * eg)], axis=2).astype(BF16))
        g_last = jnp.where(even[:, :1, :], gcol[:, CHUNK - 1:CHUNK, :], gcol[:, pr - 1:pr, :])
        hs = slice(s0, s0 + GDN_PA)
        u_ref[hs] = rhs[:, :, :LANES].reshape(GDN_PA, tt, LANES)
        w_ref[hs] = rhs[:, :, LANES:].astype(BF16).reshape(GDN_PA, tt, LANES)
        attn = (qk * decay).astype(BF16)
        at_ref[hs] = jnp.concatenate([attn[:, :, :CHUNK], attn[:, :, CHUNK:]], axis=1).reshape(GDN_PA, tt, CHUNK)
        qg_ref[hs] = (q3 * eg).astype(BF16).reshape(GDN_PA, tt, LANES)
        ktd = (kt3.astype(F32) * jnp.exp(g_last - grow)).astype(BF16)
        ktd_ref[hs] = jnp.stack([ktd[:, :, :CHUNK], ktd[:, :, CHUNK:]], axis=1).reshape(
            GDN_PA, nc, GDN_HEAD_DIM, CHUNK)
        e_last = jnp.exp(g_last)
        el_ref[hs] = jnp.stack([jnp.broadcast_to(e_last[:, :, :1], (e_last.shape[0], 1, LANES)),
                                jnp.broadcast_to(e_last[:, :, CHUNK:CHUNK + 1], (e_last.shape[0], 1, LANES))],
                               axis=1).reshape(GDN_PA, nc, 1, LANES)

    nw = nw_ref[...]

    def chunk(c, _):
        rr = pl.ds(pl.multiple_of(c * CHUNK, CHUNK), CHUNK)
        st = s_ref[...]
        st_bf = st.astype(BF16)
        v_new = u_ref[:, rr, :] - _bmm(w_ref[:, rr, :], st_bf)
        v_new_bf = v_new.astype(BF16)
        o_c = _bmm(qg_ref[:, rr, :], st_bf) + _bmm(at_ref[:, rr, :], v_new_bf)
        s_ref[...] = st * el_ref[:, c] + _bmm(ktd_ref[:, c], v_new_bf)
        o_n = o_c * lax.rsqrt(jnp.mean(o_c * o_c, axis=2, keepdims=True) + RMS_EPS) * nw
        zz = z_ref[rr, :].astype(F32)
        for s in range(hg_n):
            zs = zz[:, s * LANES:(s + 1) * LANES]
            o_ref[rr, s * LANES:(s + 1) * LANES] = (o_n[s] * (zs * jax.nn.sigmoid(zs))).astype(o_ref.dtype)
        return 0

    lax.fori_loop(0, nc, chunk, 0)


def _gdn_recurrence(qkv, kt, proj, gb, gbt, norm_w_row, bsz, t):
    tt, hg_n = GDN_TT, GDN_HG
    nc = tt // CHUNK
    qw = (hg_n // 2) * LANES
    vw = hg_n * LANES
    q_blk, v_blk, z_blk = GDN_KEY_DIM // qw, 2 * GDN_KEY_DIM // vw, GDN_QKV_DIM // vw
    return pl.pallas_call(
        _gdn_kernel,
        grid=(bsz, GDN_V_HEADS // hg_n, t // tt),
        in_specs=[pl.BlockSpec((None, tt, qw), lambda b, j, i: (b, i, j)),
                  pl.BlockSpec((None, tt, qw), lambda b, j, i: (b, i, q_blk + j)),
                  pl.BlockSpec((None, qw, tt), lambda b, j, i: (b, j, i)),
                  pl.BlockSpec((None, tt, vw), lambda b, j, i: (b, i, v_blk + j)),
                  pl.BlockSpec((None, tt, vw), lambda b, j, i: (b, i, z_blk + j)),
                  pl.BlockSpec((None, tt, LANES), lambda b, j, i: (b, i, 0)),
                  pl.BlockSpec((None, LANES, tt), lambda b, j, i: (b, 0, i)),
                  pl.BlockSpec((1, LANES), lambda b, j, i: (0, 0))],
        out_specs=pl.BlockSpec((None, tt, vw), lambda b, j, i: (b, i, j)),
        out_shape=jax.ShapeDtypeStruct((bsz, t, GDN_VAL_DIM), BF16),
        scratch_shapes=[pltpu.VMEM((hg_n, GDN_HEAD_DIM, GDN_HEAD_DIM), F32),
                        pltpu.VMEM((hg_n, tt, LANES), F32),
                        pltpu.VMEM((hg_n, tt, LANES), BF16),
                        pltpu.VMEM((hg_n, tt, CHUNK), BF16),
                        pltpu.VMEM((hg_n, tt, LANES), BF16),
                        pltpu.VMEM((hg_n, nc, GDN_HEAD_DIM, CHUNK), BF16),
                        pltpu.VMEM((hg_n, nc, 1, LANES), F32)],
        compiler_params=_cparams("parallel", "parallel", "arbitrary"),
        name="gdn_recurrence",
    )(qkv, qkv, kt, qkv, proj, gb, gbt, norm_w_row)


def _layer_norm(r, g, b):
    mu = jnp.mean(r, axis=1, keepdims=True)
    d = r - mu
    var = jnp.mean(d * d, axis=1, keepdims=True)
    return d * lax.rsqrt(var + LN_EPS) * g + b


CNT_ROWS = 32


def _post_mix_kernel(o_ref, w_ref, x_ref, g_ref, b_ref, rwh_ref, rwl_ref, rb_ref, before_ref, hx_ref, rec_ref,
                     cnt_ref, *, tm):
    @pl.when(pl.program_id(0) == 0)
    def _():
        cnt_ref[...] = jnp.zeros_like(cnt_ref)

    mix = _dot(o_ref[...], w_ref[...])
    h = _layer_norm(DEEPNORM_ALPHA * x_ref[...] + mix, g_ref[...], b_ref[...])
    hx_ref[:, :D_MODEL] = h

    h_hi = h.astype(BF16)
    h_lo = (h - h_hi.astype(F32)).astype(BF16)
    logits = _dot(h_hi, rwh_ref[...]) + _dot(h_lo, rwh_ref[...]) + _dot(h_hi, rwl_ref[...])
    scores = jax.nn.sigmoid(logits.T[:N_EXPERTS])
    biased = scores + rb_ref[...]
    sel = [biased[e:e + 1] for e in range(N_EXPERTS)]
    sc = [scores[e:e + 1] for e in range(N_EXPERTS)]
    best, best_score = None, None
    for gidx in range(N_GROUPS):
        a, b, c, d = sel[4 * gidx:4 * gidx + 4]
        p, q, r, s = jnp.maximum(a, b), jnp.minimum(a, b), jnp.maximum(c, d), jnp.minimum(c, d)
        gs = jnp.maximum(p, r) + jnp.maximum(jnp.minimum(p, r), jnp.maximum(q, s))
        if gidx == 0:
            best, best_score = jnp.zeros_like(gs, dtype=jnp.int32), gs
        else:
            upd = gs > best_score
            best = jnp.where(upd, gidx, best)
            best_score = jnp.where(upd, gs, best_score)
    pick = lambda rows: [jnp.where(best == 0, rows[j], jnp.where(best == 1, rows[4 + j], jnp.where(
        best == 2, rows[8 + j], rows[12 + j]))) for j in range(EXPERTS_PER_GROUP)]
    v, s4 = pick(sel), pick(sc)
    kept = []
    for j in range(EXPERTS_PER_GROUP):
        beaten = [jnp.where(v[i] >= v[j] if i < j else v[i] > v[j], 1, 0) for i in range(EXPERTS_PER_GROUP) if i != j]
        kept.append(beaten[0] + beaten[1] + beaten[2] < 2)
    zero = jnp.zeros_like(s4[0])
    denom = (jnp.where(kept[0], s4[0], zero) + jnp.where(kept[1], s4[1], zero)
             + jnp.where(kept[2], s4[2], zero) + jnp.where(kept[3], s4[3], zero))
    lo = jnp.where(kept[0], 0, jnp.where(kept[1], 1, 2))
    hi = jnp.where(kept[3], 3, jnp.where(kept[2], 2, 1))
    w_lo = jnp.where(kept[0], s4[0], jnp.where(kept[1], s4[1], s4[2])) / denom
    w_hi = jnp.where(kept[3], s4[3], jnp.where(kept[2], s4[2], s4[1])) / denom
    pair = jnp.where(lo == 0, hi - 1, jnp.where(lo == 1, hi + 1, 5))
    bucket = best * 6 + pair

    brow = lax.broadcasted_iota(jnp.int32, (CNT_ROWS, tm), 0)
    mine = brow == bucket
    onehot = jnp.where(mine, 1.0, 0.0)
    cnt = cnt_ref[...]
    prefix = _dot(onehot.astype(BF16), before_ref[...]) + jnp.concatenate([cnt] * (tm // LANES), axis=1)
    rank = jnp.sum(jnp.where(mine, prefix, 0.0), axis=0, keepdims=True)
    cnt_ref[...] = cnt + jnp.sum(onehot, axis=1, keepdims=True)

    row8 = lax.broadcasted_iota(jnp.int32, (8, tm), 0)
    rec = jnp.where(row8 == 0, w_lo, jnp.where(row8 == 1, w_hi, jnp.where(
        row8 == 2, bucket.astype(F32), jnp.where(row8 == 3, rank, 0.0))))
    rec_ref[...] = rec
    hx_ref[:, D_MODEL:] = jnp.concatenate([rec, jnp.zeros((TAIL - 8, tm), F32)], axis=0).T


def _post_mix(o, w_out, x, ln_g, ln_b, rw_hi, rw_lo, rb_col):
    n, ko = o.shape
    tm = 512
    kern = functools.partial(_post_mix_kernel, tm=tm)
    row = lambda i: (0, 0)
    before = (jnp.arange(tm)[:, None] < jnp.arange(tm)[None, :]).astype(BF16)
    return pl.pallas_call(
        kern,
        grid=(n // tm,),
        in_specs=[pl.BlockSpec((tm, ko), lambda i: (i, 0)),
                  pl.BlockSpec((ko, D_MODEL), row),
                  pl.BlockSpec((tm, D_MODEL), lambda i: (i, 0)),
                  pl.BlockSpec((1, D_MODEL), row),
                  pl.BlockSpec((1, D_MODEL), row),
                  pl.BlockSpec((D_MODEL, LANES), row),
                  pl.BlockSpec((D_MODEL, LANES), row),
                  pl.BlockSpec((N_EXPERTS, 1), row),
                  pl.BlockSpec((tm, tm), row)],
        out_specs=[pl.BlockSpec((tm, D_MODEL + TAIL), lambda i: (i, 0)),
                   pl.BlockSpec((8, tm), lambda i: (0, i)),
                   pl.BlockSpec((CNT_ROWS, LANES), row)],
        out_shape=[jax.ShapeDtypeStruct((n, D_MODEL + TAIL), F32),
                   jax.ShapeDtypeStruct((8, n), F32),
                   jax.ShapeDtypeStruct((CNT_ROWS, LANES), F32)],
        compiler_params=_cparams("arbitrary"),
        name="post_mix",
    )(o, w_out, x, ln_g, ln_b, rw_hi, rw_lo, rb_col, before)


PERM_CHUNK = 512
PERM_UNROLL = 8


def _row_copy(src_ref, dst_ref, sem, s, d):
    return pltpu.make_async_copy(src_ref.at[pl.ds(s, 1)], dst_ref.at[pl.ds(d, 1)], sem)


def _permute_rows_loops(copy_row):
    def issue(i, _):
        for u in range(PERM_UNROLL):
            copy_row(i * PERM_UNROLL + u).start()
        return 0

    def drain(i, _):
        for u in range(PERM_UNROLL):
            copy_row(0).wait()
        return 0

    lax.fori_loop(0, PERM_CHUNK // PERM_UNROLL, issue, 0)
    lax.fori_loop(0, PERM_CHUNK // PERM_UNROLL, drain, 0)


def _scatter_rows_kernel(idx_ref, src_ref, init_ref, out_ref, sem):
    del init_ref
    _permute_rows_loops(lambda r: _row_copy(src_ref, out_ref, sem, r, idx_ref[r]))


def _scatter_rows(src, dest, n_out):
    n, w = src.shape
    return pl.pallas_call(
        _scatter_rows_kernel,
        grid=(n // PERM_CHUNK,),
        in_specs=[pl.BlockSpec((PERM_CHUNK,), lambda i: (i,), memory_space=pltpu.SMEM),
                  pl.BlockSpec((PERM_CHUNK, w), lambda i: (i, 0)),
                  pl.BlockSpec(memory_space=pl.ANY)],
        out_specs=pl.BlockSpec(memory_space=pl.ANY),
        out_shape=jax.ShapeDtypeStruct((n_out, w), src.dtype),
        scratch_shapes=[pltpu.SemaphoreType.DMA(())],
        input_output_aliases={2: 0},
        compiler_params=pltpu.CompilerParams(dimension_semantics=("arbitrary",), has_side_effects=True),
        name="scatter_rows",
    )(dest, src, jnp.zeros((n_out, w), src.dtype))


def _gather_rows_kernel(idx_ref, src_ref, out_ref, sem):
    _permute_rows_loops(lambda r: _row_copy(src_ref, out_ref, sem, idx_ref[r], r))


def _gather_rows(src, idx):
    n = idx.shape[0]
    w = src.shape[1]
    return pl.pallas_call(
        _gather_rows_kernel,
        grid=(n // PERM_CHUNK,),
        in_specs=[pl.BlockSpec((PERM_CHUNK,), lambda i: (i,), memory_space=pltpu.SMEM),
                  pl.BlockSpec(memory_space=pl.ANY)],
        out_specs=pl.BlockSpec((PERM_CHUNK, w), lambda i: (i, 0)),
        out_shape=jax.ShapeDtypeStruct((n, w), src.dtype),
        scratch_shapes=[pltpu.SemaphoreType.DMA(())],
        compiler_params=pltpu.CompilerParams(dimension_semantics=("arbitrary",), has_side_effects=True),
        name="gather_rows",
    )(idx, src)


MOE_TILE = 256


def _moe_kernel(src_ref, ea_ref, eb_ref, nused_ref, xs_ref, gua_ref, gub_ref, da_ref, db_ref, y_ref):
    del src_ref, ea_ref, eb_ref

    @pl.when(pl.program_id(0) >= nused_ref[0])
    def _():
        y_ref[...] = jnp.zeros_like(y_ref)

    @pl.when(pl.program_id(0) < nused_ref[0])
    def _():
        x = xs_ref[:, :D_MODEL].astype(BF16)
        tail = xs_ref[:, D_MODEL:]
        lane = lax.broadcasted_iota(jnp.int32, tail.shape, 1)
        w_lo = jnp.sum(jnp.where(lane == 0, tail, 0.0), axis=1, keepdims=True)
        w_hi = jnp.sum(jnp.where(lane == 1, tail, 0.0), axis=1, keepdims=True)

        def expert(gu_ref, d_ref):
            gu = _dot(x, gu_ref[...])
            g_in, u_in = gu[:, :D_EXPERT], gu[:, D_EXPERT:]
            act = (g_in * jax.nn.sigmoid(g_in)) * u_in
            return _dot(act.astype(BF16), d_ref[...])

        y_ref[...] = w_lo * expert(gua_ref, da_ref) + w_hi * expert(gub_ref, db_ref)


def _moe_ffn(xs, tile_src, tile_ea, tile_eb, n_used, w_gate_up, w_down):
    n_pad = xs.shape[0]
    n_tiles = n_pad // MOE_TILE
    grid_spec = pltpu.PrefetchScalarGridSpec(
        num_scalar_prefetch=4,
        grid=(n_tiles,),
        in_specs=[pl.BlockSpec((MOE_TILE, D_MODEL + TAIL), lambda t, src, ea, eb, nu: (src[t], 0)),
                  pl.BlockSpec((None, D_MODEL, 2 * D_EXPERT), lambda t, src, ea, eb, nu: (ea[t], 0, 0)),
                  pl.BlockSpec((None, D_MODEL, 2 * D_EXPERT), lambda t, src, ea, eb, nu: (eb[t], 0, 0)),
                  pl.BlockSpec((None, D_EXPERT, D_MODEL), lambda t, src, ea, eb, nu: (ea[t], 0, 0)),
                  pl.BlockSpec((None, D_EXPERT, D_MODEL), lambda t, src, ea, eb, nu: (eb[t], 0, 0))],
        out_specs=pl.BlockSpec((MOE_TILE, D_MODEL), lambda t, src, ea, eb, nu: (t, 0)),
    )
    return pl.pallas_call(
        _moe_kernel,
        grid_spec=grid_spec,
        out_shape=jax.ShapeDtypeStruct((n_pad, D_MODEL), F32),
        compiler_params=_cparams("arbitrary"),
        name="moe_ffn",
    )(tile_src, tile_ea, tile_eb, n_used, xs, w_gate_up, w_gate_up, w_down, w_down)


def _routing_plan(rec, counts, n):
    n_tiles = n // MOE_TILE + N_BUCKETS
    cnt = counts[:N_BUCKETS, 0].astype(jnp.int32)
    tiles_per = (cnt + MOE_TILE - 1) // MOE_TILE
    tile_end = jnp.cumsum(tiles_per)
    tile_start = tile_end - tiles_per
    bucket = rec[2].astype(jnp.int32)
    rank = rec[3].astype(jnp.int32)
    dest = tile_start[bucket] * MOE_TILE + rank
    n_used = tile_end[-1]
    tile_id = jnp.clip(jnp.arange(n_tiles, dtype=jnp.int32), 0, jnp.maximum(n_used - 1, 0))
    tile_bucket = jnp.sum(tile_id[:, None] >= tile_end[None, :], axis=1).astype(jnp.int32)
    group = tile_bucket // 6
    pair = tile_bucket % 6
    lo = jnp.where(pair < 3, 0, jnp.where(pair < 5, 1, 2))
    hi = jnp.where(pair < 3, pair + 1, jnp.where(pair < 5, pair - 1, 3))
    return dest, tile_id, group * 4 + lo, group * 4 + hi, n_used.reshape(1).astype(jnp.int32), n_tiles * MOE_TILE


def _final_kernel(hx_ref, moe_ref, p_ref, wg_ref, wp_ref, g_ref, b_ref, o_ref):
    h = hx_ref[...]
    gate = jax.nn.sigmoid(_dot(h.astype(BF16), wg_ref[...]))
    ple = gate * _dot(p_ref[...].astype(BF16), wp_ref[...])
    o_ref[...] = _layer_norm(DEEPNORM_ALPHA * h + moe_ref[...] + ple, g_ref[...], b_ref[...])


def _final(hx, moe, p, w_gate, w_proj, ln_g, ln_b):
    n = moe.shape[0]
    tm = 512
    row = lambda i: (0, 0)
    return pl.pallas_call(
        _final_kernel,
        grid=(n // tm,),
        in_specs=[pl.BlockSpec((tm, D_MODEL), lambda i: (i, 0)),
                  pl.BlockSpec((tm, D_MODEL), lambda i: (i, 0)),
                  pl.BlockSpec((tm, PLE_DIM), lambda i: (i, 0)),
                  pl.BlockSpec((D_MODEL, D_MODEL), row),
                  pl.BlockSpec((PLE_DIM, D_MODEL), row),
                  pl.BlockSpec((1, D_MODEL), row),
                  pl.BlockSpec((1, D_MODEL), row)],
        out_specs=pl.BlockSpec((tm, D_MODEL), lambda i: (i, 0)),
        out_shape=jax.ShapeDtypeStruct((n, D_MODEL), F32),
        compiler_params=_cparams("parallel"),
        name="final",
    )(hx, moe, p, w_gate, w_proj, ln_g, ln_b)


def _pad_lanes(a, width=LANES):
    return jnp.pad(a, [(0, 0)] * (a.ndim - 1) + [(0, width - a.shape[-1])])


def _moe_and_norm(o, w_out, x, p_i, ln1_g, ln1_b, router, w_gate_up, w_down, ple_w_gate, ple_w_proj,
                  ln2_g, ln2_b):
    n = x.shape[0]
    hx, rec, counts = _post_mix(o, w_out.astype(BF16), x, ln1_g[None], ln1_b[None], *router)
    dest, tile_src, tile_ea, tile_eb, n_used, n_pad = _routing_plan(rec, counts, n)
    xs = _scatter_rows(hx, dest, n_pad)
    ys = _moe_ffn(xs, tile_src, tile_ea, tile_eb, n_used, w_gate_up.astype(BF16), w_down.astype(BF16))
    moe = _gather_rows(ys, dest)
    return _final(hx, moe, p_i, ple_w_gate.astype(BF16), ple_w_proj.astype(BF16), ln2_g[None], ln2_b[None])


def kernel(x, p, sb_w_in, sb_w_out, gdn_w_in, gdn_conv_w, gdn_a_log, gdn_dt_bias, gdn_norm_w, gdn_w_out,
           ln1_g, ln1_b, router_w, router_b, expert_w_gate_up, expert_w_down, ple_w_gate, ple_w_proj,
           ln2_g, ln2_b):
    bsz, t, d = x.shape
    n = bsz * t
    x = x.reshape(n, d)
    p = p.reshape(DEPTH, n, PLE_DIM)
    rw_pad = _pad_lanes(router_w)
    rw_hi = rw_pad.astype(BF16)
    router = (rw_hi, (rw_pad - rw_hi.astype(F32)).astype(BF16), router_b[:, None])

    qkv = _linear(x, sb_w_in[0].astype(BF16), BF16, 1024, 1024)
    o = _sb_attention(qkv.reshape(bsz, t, 3 * d), bsz, t).reshape(n, d)
    x = _moe_and_norm(o, sb_w_out[0], x, p[0], ln1_g[0], ln1_b[0], router, expert_w_gate_up[0],
                      expert_w_down[0], ple_w_gate[0], ple_w_proj[0], ln2_g[0], ln2_b[0])

    w_in = gdn_w_in[0]
    split = GDN_QKV_DIM + GDN_VAL_DIM
    proj = _linear(x, w_in[:, :split].astype(BF16), BF16, 1024, 1024)
    ba = _linear(x, _pad_lanes(w_in[:, split:]).astype(BF16), F32, 1024, LANES)
    proj = proj.reshape(bsz, t, split)
    qkv_c, kt = _gdn_conv(proj, gdn_conv_w[0], bsz, t)
    head_pad = lambda a: jnp.pad(a, (GDN_V_HEADS, LANES - 2 * GDN_V_HEADS))[None]
    gb, gbt = _gdn_gate_prep(ba.reshape(bsz, t, LANES), head_pad(gdn_a_log[0]), head_pad(gdn_dt_bias[0]), bsz, t)
    o = _gdn_recurrence(qkv_c, kt, proj, gb, gbt, gdn_norm_w[0][None], bsz, t).reshape(n, GDN_VAL_DIM)
    x = _moe_and_norm(o, gdn_w_out[0], x, p[1], ln1_g[1], ln1_b[1], router, expert_w_gate_up[1],
                      expert_w_down[1], ple_w_gate[1], ple_w_proj[1], ln2_g[1], ln2_b[1])
    return x.reshape(bsz, t, d)
```

```python
import functools

import jax
import jax.numpy as jnp
from jax import lax
from jax.experimental import pallas as pl
from jax.experimental.pallas import tpu as pltpu

D_MODEL = 1024
DEPTH = 2
PLE_DIM = 256
SB_HEAD_DIM = 64
GDN_QK_HEADS = 8
GDN_V_HEADS = 16
GDN_HEAD_DIM = 128
GDN_KEY_DIM = GDN_QK_HEADS * GDN_HEAD_DIM
GDN_VAL_DIM = GDN_V_HEADS * GDN_HEAD_DIM
GDN_QKV_DIM = 2 * GDN_KEY_DIM + GDN_VAL_DIM
CONV_K = 4
CHUNK = 64
N_EXPERTS = 16
N_GROUPS = 4
EXPERTS_PER_GROUP = 4
D_EXPERT = 512
N_BUCKETS = 24
LN_EPS = 1e-5
RMS_EPS = 1e-6
DEEPNORM_ALPHA = (2 * DEPTH) ** 0.25

LANES = 128
TAIL = LANES
VMEM_LIMIT = 48 * 1024 * 1024
SB_DEAD_LOG2 = -151.0
LOG2_E = 1.4426950408889634

F32 = jnp.float32
BF16 = jnp.bfloat16


def _cparams(*sem):
    return pltpu.CompilerParams(dimension_semantics=sem, vmem_limit_bytes=VMEM_LIMIT)


def _dot(a, b):
    return jnp.dot(a, b, preferred_element_type=F32)


def _linear_kernel(x_ref, w_ref, o_ref):
    o_ref[...] = _dot(x_ref[...].astype(BF16), w_ref[...]).astype(o_ref.dtype)


def _linear(x, w, out_dtype, tm, tn):
    m, k = x.shape
    n = w.shape[1]
    return pl.pallas_call(
        _linear_kernel,
        grid=(m // tm, n // tn),
        in_specs=[pl.BlockSpec((tm, k), lambda i, j: (i, 0)),
                  pl.BlockSpec((k, tn), lambda i, j: (0, j))],
        out_specs=pl.BlockSpec((tm, tn), lambda i, j: (i, j)),
        out_shape=jax.ShapeDtypeStruct((m, n), out_dtype),
        compiler_params=_cparams("parallel", "parallel"),
        name="linear",
    )(x, w)


SB_TQ = 128
SB_TK = 128
SB_GROUP = 8
SB_PIECE = 64


def _sb_kernel(q_ref, k_ref, v_ref, o_ref, carry_ref, acc_ref, *, scale):
    t = q_ref.shape[0]
    tq, tk, grp = SB_TQ, SB_TK, SB_GROUP
    rows = 2 * tq
    nrow = grp * rows
    npiece = nrow // SB_PIECE
    lane = lax.broadcasted_iota(jnp.int32, (tq, LANES), 1)
    first_head = lane < SB_HEAD_DIM
    jj = lax.broadcasted_iota(jnp.int32, (2 * tk, 2 * tk), 0) % tk
    ss = lax.broadcasted_iota(jnp.int32, (2 * tk, 2 * tk), 1)
    cum_mat = jnp.where((jj > ss) | (ss >= tk), 1.0, 0.0).astype(BF16)
    rp = lax.broadcasted_iota(jnp.int32, (SB_PIECE, tk), 0)
    cp = lax.broadcasted_iota(jnp.int32, (SB_PIECE, tk), 1)
    pieces_per_half = tq // SB_PIECE
    diag_mask = [cp < rp + (p % pieces_per_half) * SB_PIECE for p in range(pieces_per_half)]
    piece = lambda a, p: a[p * SB_PIECE:(p + 1) * SB_PIECE]

    def group(gi, _):
        first_tile = gi * grp
        qs = []
        for a in range(grp):
            q = q_ref[pl.ds(pl.multiple_of((first_tile + a) * tq, tq), tq), :] * scale
            zero = jnp.zeros_like(q)
            qs += [jnp.where(first_head, q, zero), jnp.where(first_head, zero, q)]
        qq = jnp.concatenate(qs, axis=0).reshape(grp, rows, LANES)
        carry_ref[...] = jnp.zeros_like(carry_ref)
        acc_ref[...] = jnp.zeros_like(acc_ref)

        def alive(c):
            d, top = c
            return jnp.logical_and(d < first_tile + grp, top >= SB_DEAD_LOG2)

        def step(d, diagonal):
            ks, vs = [], []
            for a in range(grp):
                k0 = pl.multiple_of(jnp.maximum(first_tile + a - d, 0) * tk, tk)
                ks.append(k_ref[pl.ds(k0, tk), :])
                vs.append(v_ref[pl.ds(k0, tk), :])
            z = jnp.einsum("gqd,gkd->gqk", qq, jnp.stack(ks), preferred_element_type=F32).reshape(nrow, tk)
            hls, lps = [], []
            for p in range(npiece):
                z2 = piece(z, p) * LOG2_E
                lb = jnp.minimum(z2, 0.0) - jnp.log2(1.0 + jnp.exp2(-jnp.abs(z2)))
                lm = lb - z2
                if diagonal:
                    mp = diag_mask[p % pieces_per_half]
                else:
                    mp = first_tile + p // (2 * pieces_per_half) - d >= 0
                lm = jnp.where(mp, lm, 0.0)
                hi = lm.astype(BF16)
                hls.append(jnp.concatenate([hi, (lm - hi.astype(F32)).astype(BF16)], axis=1))
                lps.append(jnp.where(mp, lb, -1e30))
            cs = _dot(jnp.concatenate(hls, axis=0), cum_mat)
            attn, top = [], None
            for p in range(npiece):
                rr = pl.ds(p * SB_PIECE, SB_PIECE)
                csp = piece(cs, p)
                carry = carry_ref[rr, :]
                attn.append(jnp.exp2(lps[p] + csp[:, :tk] + carry).astype(BF16))
                carry = carry + csp[:, tk:]
                if not diagonal:
                    carry = jnp.where(first_tile + p // (2 * pieces_per_half) - d >= 0, carry, -1e30)
                carry_ref[rr, :] = carry
                top = carry if top is None else jnp.maximum(top, carry)
            av = jnp.einsum("gqk,gkd->gqd", jnp.concatenate(attn, axis=0).reshape(grp, rows, tk), jnp.stack(vs),
                            preferred_element_type=F32)
            acc_ref[...] += av.reshape(nrow, LANES)
            return d + 1, jnp.max(top)

        lax.while_loop(alive, lambda c: step(c[0], False), step(jnp.int32(0), True))
        for a in range(grp):
            acc = acc_ref[a * rows:(a + 1) * rows, :]
            o_ref[pl.ds(pl.multiple_of((first_tile + a) * tq, tq), tq), :] = jnp.where(
                first_head, acc[:tq], acc[tq:]).astype(o_ref.dtype)
        return 0

    lax.fori_loop(0, t // (tq * grp), group, 0)


def _sb_attention(qkv, bsz, t):
    ncol = D_MODEL // LANES
    nrow = SB_GROUP * 2 * SB_TQ
    kern = functools.partial(_sb_kernel, scale=SB_HEAD_DIM ** -0.5)
    return pl.pallas_call(
        kern,
        grid=(bsz, ncol),
        in_specs=[pl.BlockSpec((None, t, LANES), lambda b, h: (b, 0, h)),
                  pl.BlockSpec((None, t, LANES), lambda b, h: (b, 0, ncol + h)),
                  pl.BlockSpec((None, t, LANES), lambda b, h: (b, 0, 2 * ncol + h))],
        out_specs=pl.BlockSpec((None, t, LANES), lambda b, h: (b, 0, h)),
        out_shape=jax.ShapeDtypeStruct((bsz, t, D_MODEL), BF16),
        scratch_shapes=[pltpu.VMEM((nrow, SB_TK), F32), pltpu.VMEM((nrow, LANES), F32)],
        compiler_params=_cparams("parallel", "parallel"),
        name="sb_attention",
    )(qkv, qkv, qkv)


CONV_PIECE = 128


def _conv_kernel(x_ref, halo_ref, w_ref, o_ref, kt_ref, *, tt, tc):
    i = pl.program_id(1)
    j = pl.program_id(2)
    w = w_ref[...]
    q_blocks = GDN_KEY_DIM // tc

    def run(normalise, transpose):
        for pc in range(tt // CONV_PIECE):
            rows = slice(pc * CONV_PIECE, (pc + 1) * CONV_PIECE)
            if pc == 0:
                prev = jnp.where(i > 0, halo_ref[...].astype(F32), 0.0)
            else:
                prev = x_ref[pc * CONV_PIECE - 8:pc * CONV_PIECE, :].astype(F32)
            cur = x_ref[rows, :].astype(F32)
            ext = jnp.concatenate([prev, cur], axis=0)
            acc = cur * w[CONV_K - 1:CONV_K, :]
            for s in range(1, CONV_K):
                acc = acc + pltpu.roll(ext, s, 0)[8:] * w[CONV_K - 1 - s:CONV_K - s, :]
            y = acc * jax.nn.sigmoid(acc)
            if normalise:
                parts = []
                for h in range(tc // GDN_HEAD_DIM):
                    yh = y[:, h * GDN_HEAD_DIM:(h + 1) * GDN_HEAD_DIM]
                    parts.append(yh * lax.rsqrt(jnp.sum(yh * yh, axis=1, keepdims=True) + RMS_EPS))
                y = jnp.concatenate(parts, axis=1)
            o_ref[rows, :] = y.astype(o_ref.dtype)
            if transpose:
                kt_ref[:, rows] = y.T.astype(kt_ref.dtype)

    @pl.when(j < q_blocks)
    def _():
        run(True, False)

    @pl.when((j >= q_blocks) & (j < 2 * q_blocks))
    def _():
        run(True, True)

    @pl.when(j >= 2 * q_blocks)
    def _():
        run(False, False)


def _gdn_conv(proj, conv_w, bsz, t):
    tt, tc = 2048, 256
    kern = functools.partial(_conv_kernel, tt=tt, tc=tc)
    q_blocks = GDN_KEY_DIM // tc
    return pl.pallas_call(
        kern,
        grid=(bsz, t // tt, GDN_QKV_DIM // tc),
        in_specs=[pl.BlockSpec((None, tt, tc), lambda b, i, j: (b, i, j)),
                  pl.BlockSpec((None, 8, tc), lambda b, i, j: (b, jnp.maximum(i * (tt // 8) - 1, 0), j)),
                  pl.BlockSpec((CONV_K, tc), lambda b, i, j: (0, j))],
        out_specs=[pl.BlockSpec((None, tt, tc), lambda b, i, j: (b, i, j)),
                   pl.BlockSpec((None, tc, tt), lambda b, i, j: (b, jnp.clip(j - q_blocks, 0, q_blocks - 1), i))],
        out_shape=[jax.ShapeDtypeStruct((bsz, t, GDN_QKV_DIM), BF16),
                   jax.ShapeDtypeStruct((bsz, GDN_KEY_DIM, t), BF16)],
        compiler_params=_cparams("parallel", "parallel", "arbitrary"),
        name="gdn_conv",
    )(proj, proj, conv_w)


def _gate_prep_kernel(ba_ref, alog_ref, dtb_ref, gb_ref, gbt_ref, *, tt):
    ba = ba_ref[...]
    lane = lax.broadcasted_iota(jnp.int32, (tt, LANES), 1)
    beta = jax.nn.sigmoid(ba)
    xa = ba + dtb_ref[...]
    softplus = jnp.maximum(xa, 0.0) + jnp.log1p(jnp.exp(-jnp.abs(xa)))
    g = -jnp.exp(alog_ref[...]) * softplus
    r = lax.broadcasted_iota(jnp.int32, (tt, tt), 0)
    c = lax.broadcasted_iota(jnp.int32, (tt, tt), 1)
    tri = jnp.where((c <= r) & (c // CHUNK == r // CHUNK), 1.0, 0.0).astype(BF16)
    is_g = (lane >= GDN_V_HEADS) & (lane < 2 * GDN_V_HEADS)
    g = jnp.where(is_g, g, 0.0)
    rem = g
    gc = jnp.zeros_like(g)
    for _ in range(3):
        hi = rem.astype(BF16)
        gc = gc + _dot(tri, hi)
        rem = rem - hi.astype(F32)
    out = jnp.where(lane < GDN_V_HEADS, beta, gc)
    gb_ref[...] = out
    gbt_ref[...] = out.T


def _gdn_gate_prep(ba, a_log_row, dt_bias_row, bsz, t):
    tt = 512
    kern = functools.partial(_gate_prep_kernel, tt=tt)
    return pl.pallas_call(
        kern,
        grid=(bsz, t // tt),
        in_specs=[pl.BlockSpec((None, tt, LANES), lambda b, i: (b, i, 0)),
                  pl.BlockSpec((1, LANES), lambda b, i: (0, 0)),
                  pl.BlockSpec((1, LANES), lambda b, i: (0, 0))],
        out_specs=[pl.BlockSpec((None, tt, LANES), lambda b, i: (b, i, 0)),
                   pl.BlockSpec((None, LANES, tt), lambda b, i: (b, 0, i))],
        out_shape=[jax.ShapeDtypeStruct((bsz, t, LANES), F32),
                   jax.ShapeDtypeStruct((bsz, LANES, t), F32)],
        compiler_params=_cparams("parallel", "parallel"),
        name="gdn_gate_prep",
    )(ba, a_log_row, dt_bias_row)


GDN_TT = 512
GDN_HG = 16
GDN_PA = 4


def _bmm(a, b):
    return jnp.einsum("cik,ckj->cij", a, b, preferred_element_type=F32)


def _gdn_kernel(q_ref, k_ref, kt_ref, v_ref, z_ref, gb_ref, gbt_ref, nw_ref, o_ref,
                s_ref, u_ref, w_ref, at_ref, qg_ref, ktd_ref, el_ref):
    tt, hg_n = GDN_TT, GDN_HG
    nc = tt // CHUNK
    hg = pl.program_id(1)

    @pl.when(pl.program_id(2) == 0)
    def _():
        s_ref[...] = jnp.zeros_like(s_ref)

    lane = lax.broadcasted_iota(jnp.int32, (tt, LANES), 1)
    sub8 = lax.broadcasted_iota(jnp.int32, (hg_n, tt), 0)
    npair = nc // 2
    pr = 2 * CHUNK
    ri = lax.broadcasted_iota(jnp.int32, (1, CHUNK, pr), 1)
    ci = lax.broadcasted_iota(jnp.int32, (1, CHUNK, pr), 2)
    even = ci < CHUNK
    cj = ci % CHUNK
    lower_incl = ri >= cj
    strict = ri > cj
    eye = jnp.where(ri == cj, 1.0, 0.0)
    qscale = GDN_HEAD_DIM ** -0.5
    gb = gb_ref[...]
    rows8 = gbt_ref[pl.ds(pl.multiple_of(GDN_V_HEADS + hg * hg_n, 8), hg_n), :]
    pairs = lambda a: a.reshape(npair, pr, a.shape[-1])
    lane_pairs = lambda a: jnp.stack([a[:, m * pr:(m + 1) * pr] for m in range(npair)])
    side = lambda x: jnp.where(even, x[:, :CHUNK, :], x[:, CHUNK:, :])
    zero_bf = jnp.zeros((1, CHUNK, pr), BF16)
    block_diag = lambda x: jnp.concatenate([jnp.where(even, x, zero_bf), jnp.where(even, zero_bf, x)], axis=1)

    cat = lambda xs: jnp.concatenate(xs, axis=0)
    for s0 in range(0, hg_n, GDN_PA):
        heads = range(s0, s0 + GDN_PA)
        q3s, k3s, kt3s, kks, qks = {}, {}, {}, {}, {}
        for qh in sorted({s // 2 for s in heads}):
            cols = slice(qh * LANES, (qh + 1) * LANES)
            q3s[qh] = pairs(q_ref[:, cols].astype(F32) * qscale)
            k3s[qh] = pairs(k_ref[:, cols].astype(F32))
            kt3s[qh] = lane_pairs(kt_ref[cols, :])
            kks[qh] = side(_bmm(k3s[qh].astype(BF16), kt3s[qh]))
            qks[qh] = side(_bmm(q3s[qh].astype(BF16), kt3s[qh]))
        per_head = lambda d: cat([d[s // 2] for s in heads])
        q3, k3, kt3, kk, qk = per_head(q3s), per_head(k3s), per_head(kt3s), per_head(kks), per_head(qks)
        bcol = cat([pairs(jnp.sum(jnp.where(lane == hg * hg_n + s, gb, 0.0), axis=1, keepdims=True)) for s in heads])
        gcol = cat([pairs(jnp.sum(jnp.where(lane == GDN_V_HEADS + hg * hg_n + s, gb, 0.0), axis=1, keepdims=True))
                    for s in heads])
        grow = cat([lane_pairs(jnp.sum(jnp.where(sub8 == s, rows8, 0.0), axis=0, keepdims=True))
                    for s in heads])
        v3 = cat([pairs(v_ref[:, s * LANES:(s + 1) * LANES].astype(F32)) for s in heads])
        gcol = jnp.broadcast_to(gcol, (gcol.shape[0], pr, pr))
        bcol = jnp.broadcast_to(bcol, (bcol.shape[0], pr, pr))
        decay = jnp.where(lower_incl, jnp.exp(jnp.minimum(side(gcol) - grow, 0.0)), 0.0)
        a = jnp.where(strict, side(bcol) * kk * decay, 0.0)
        eg = jnp.exp(gcol)
        p = a.astype(BF16)
        t_inv = eye - a
        for _ in range(5):
            p = _bmm(p, block_diag(p)).astype(BF16)
            t_inv = t_inv + _bmm(t_inv.astype(BF16), block_diag(p))
        rhs = _bmm(block_diag(t_inv.astype(BF16)),
                   jnp.concatenate([v3 * bcol, k3 * (bcol * eg)], axis=2).astype(BF16))
        g_last = jnp.where(even[:, :1, :], gcol[:, CHUNK - 1:CHUNK, :], gcol[:, pr - 1:pr, :])
        hs = slice(s0, s0 + GDN_PA)
        u_ref[hs] = rhs[:, :, :LANES].reshape(GDN_PA, tt, LANES)
        w_ref[hs] = rhs[:, :, LANES:].astype(BF16).reshape(GDN_PA, tt, LANES)
        attn = (qk * decay).astype(BF16)
        at_ref[hs] = jnp.concatenate([attn[:, :, :CHUNK], attn[:, :, CHUNK:]], axis=1).reshape(GDN_PA, tt, CHUNK)
        qg_ref[hs] = (q3 * eg).astype(BF16).reshape(GDN_PA, tt, LANES)
        ktd = (kt3.astype(F32) * jnp.exp(g_last - grow)).astype(BF16)
        ktd_ref[hs] = jnp.stack([ktd[:, :, :CHUNK], ktd[:, :, CHUNK:]], axis=1).reshape(
            GDN_PA, nc, GDN_HEAD_DIM, CHUNK)
        e_last = jnp.exp(g_last)
        el_ref[hs] = jnp.stack([jnp.broadcast_to(e_last[:, :, :1], (e_last.shape[0], 1, LANES)),
                                jnp.broadcast_to(e_last[:, :, CHUNK:CHUNK + 1], (e_last.shape[0], 1, LANES))],
                               axis=1).reshape(GDN_PA, nc, 1, LANES)

    nw = nw_ref[...]

    def chunk(c, _):
        rr = pl.ds(pl.multiple_of(c * CHUNK, CHUNK), CHUNK)
        st = s_ref[...]
        st_bf = st.astype(BF16)
        v_new = u_ref[:, rr, :] - _bmm(w_ref[:, rr, :], st_bf)
        v_new_bf = v_new.astype(BF16)
        o_c = _bmm(qg_ref[:, rr, :], st_bf) + _bmm(at_ref[:, rr, :], v_new_bf)
        s_ref[...] = st * el_ref[:, c] + _bmm(ktd_ref[:, c], v_new_bf)
        o_n = o_c * lax.rsqrt(jnp.mean(o_c * o_c, axis=2, keepdims=True) + RMS_EPS) * nw
        zz = z_ref[rr, :].astype(F32)
        for s in range(hg_n):
            zs = zz[:, s * LANES:(s + 1) * LANES]
            o_ref[rr, s * LANES:(s + 1) * LANES] = (o_n[s] * (zs * jax.nn.sigmoid(zs))).astype(o_ref.dtype)
        return 0

    lax.fori_loop(0, nc, chunk, 0)


def _gdn_recurrence(qkv, kt, proj, gb, gbt, norm_w_row, bsz, t):
    tt, hg_n = GDN_TT, GDN_HG
    nc = tt // CHUNK
    qw = (hg_n // 2) * LANES
    vw = hg_n * LANES
    q_blk, v_blk, z_blk = GDN_KEY_DIM // qw, 2 * GDN_KEY_DIM // vw, GDN_QKV_DIM // vw
    return pl.pallas_call(
        _gdn_kernel,
        grid=(bsz, GDN_V_HEADS // hg_n, t // tt),
        in_specs=[pl.BlockSpec((None, tt, qw), lambda b, j, i: (b, i, j)),
                  pl.BlockSpec((None, tt, qw), lambda b, j, i: (b, i, q_blk + j)),
                  pl.BlockSpec((None, qw, tt), lambda b, j, i: (b, j, i)),
                  pl.BlockSpec((None, tt, vw), lambda b, j, i: (b, i, v_blk + j)),
                  pl.BlockSpec((None, tt, vw), lambda b, j, i: (b, i, z_blk + j)),
                  pl.BlockSpec((None, tt, LANES), lambda b, j, i: (b, i, 0)),
                  pl.BlockSpec((None, LANES, tt), lambda b, j, i: (b, 0, i)),
                  pl.BlockSpec((1, LANES), lambda b, j, i: (0, 0))],
        out_specs=pl.BlockSpec((None, tt, vw), lambda b, j, i: (b, i, j)),
        out_shape=jax.ShapeDtypeStruct((bsz, t, GDN_VAL_DIM), BF16),
        scratch_shapes=[pltpu.VMEM((hg_n, GDN_HEAD_DIM, GDN_HEAD_DIM), F32),
                        pltpu.VMEM((hg_n, tt, LANES), F32),
                        pltpu.VMEM((hg_n, tt, LANES), BF16),
                        pltpu.VMEM((hg_n, tt, CHUNK), BF16),
                        pltpu.VMEM((hg_n, tt, LANES), BF16),
                        pltpu.VMEM((hg_n, nc, GDN_HEAD_DIM, CHUNK), BF16),
                        pltpu.VMEM((hg_n, nc, 1, LANES), F32)],
        compiler_params=_cparams("parallel", "parallel", "arbitrary"),
        name="gdn_recurrence",
    )(qkv, qkv, kt, qkv, proj, gb, gbt, norm_w_row)


def _layer_norm(r, g, b):
    mu = jnp.mean(r, axis=1, keepdims=True)
    d = r - mu
    var = jnp.mean(d * d, axis=1, keepdims=True)
    return d * lax.rsqrt(var + LN_EPS) * g + b


CNT_ROWS = 32


def _post_mix_kernel(o_ref, w_ref, x_ref, g_ref, b_ref, rwh_ref, rwl_ref, rb_ref, before_ref, hx_ref, rec_ref,
                     cnt_ref, *, tm):
    @pl.when(pl.program_id(0) == 0)
    def _():
        cnt_ref[...] = jnp.zeros_like(cnt_ref)

    mix = _dot(o_ref[...], w_ref[...])
    h = _layer_norm(DEEPNORM_ALPHA * x_ref[...] + mix, g_ref[...], b_ref[...])
    hx_ref[:, :D_MODEL] = h

    h_hi = h.astype(BF16)
    h_lo = (h - h_hi.astype(F32)).astype(BF16)
    logits = _dot(h_hi, rwh_ref[...]) + _dot(h_lo, rwh_ref[...]) + _dot(h_hi, rwl_ref[...])
    scores = jax.nn.sigmoid(logits.T[:N_EXPERTS])
    biased = scores + rb_ref[...]
    sel = [biased[e:e + 1] for e in range(N_EXPERTS)]
    sc = [scores[e:e + 1] for e in range(N_EXPERTS)]
    best, best_score = None, None
    for gidx in range(N_GROUPS):
        a, b, c, d = sel[4 * gidx:4 * gidx + 4]
        p, q, r, s = jnp.maximum(a, b), jnp.minimum(a, b), jnp.maximum(c, d), jnp.minimum(c, d)
        gs = jnp.maximum(p, r) + jnp.maximum(jnp.minimum(p, r), jnp.maximum(q, s))
        if gidx == 0:
            best, best_score = jnp.zeros_like(gs, dtype=jnp.int32), gs
        else:
            upd = gs > best_score
            best = jnp.where(upd, gidx, best)
            best_score = jnp.where(upd, gs, best_score)
    pick = lambda rows: [jnp.where(best == 0, rows[j], jnp.where(best == 1, rows[4 + j], jnp.where(
        best == 2, rows[8 + j], rows[12 + j]))) for j in range(EXPERTS_PER_GROUP)]
    v, s4 = pick(sel), pick(sc)
    kept = []
    for j in range(EXPERTS_PER_GROUP):
        beaten = [jnp.where(v[i] >= v[j] if i < j else v[i] > v[j], 1, 0) for i in range(EXPERTS_PER_GROUP) if i != j]
        kept.append(beaten[0] + beaten[1] + beaten[2] < 2)
    zero = jnp.zeros_like(s4[0])
    denom = (jnp.where(kept[0], s4[0], zero) + jnp.where(kept[1], s4[1], zero)
             + jnp.where(kept[2], s4[2], zero) + jnp.where(kept[3], s4[3], zero))
    lo = jnp.where(kept[0], 0, jnp.where(kept[1], 1, 2))
    hi = jnp.where(kept[3], 3, jnp.where(kept[2], 2, 1))
    w_lo = jnp.where(kept[0], s4[0], jnp.where(kept[1], s4[1], s4[2])) / denom
    w_hi = jnp.where(kept[3], s4[3], jnp.where(kept[2], s4[2], s4[1])) / denom
    pair = jnp.where(lo == 0, hi - 1, jnp.where(lo == 1, hi + 1, 5))
    bucket = best * 6 + pair

    brow = lax.broadcasted_iota(jnp.int32, (CNT_ROWS, tm), 0)
    mine = brow == bucket
    onehot = jnp.where(mine, 1.0, 0.0)
    cnt = cnt_ref[...]
    prefix = _dot(onehot.astype(BF16), before_ref[...]) + jnp.concatenate([cnt] * (tm // LANES), axis=1)
    rank = jnp.sum(jnp.where(mine, prefix, 0.0), axis=0, keepdims=True)
    cnt_ref[...] = cnt + jnp.sum(onehot, axis=1, keepdims=True)

    row8 = lax.broadcasted_iota(jnp.int32, (8, tm), 0)
    rec = jnp.where(row8 == 0, w_lo, jnp.where(row8 == 1, w_hi, jnp.where(
        row8 == 2, bucket.astype(F32), jnp.where(row8 == 3, rank, 0.0))))
    rec_ref[...] = rec
    hx_ref[:, D_MODEL:] = jnp.concatenate([rec, jnp.zeros((TAIL - 8, tm), F32)], axis=0).T


def _post_mix(o, w_out, x, ln_g, ln_b, rw_hi, rw_lo, rb_col):
    n, ko = o.shape
    tm = 512
    kern = functools.partial(_post_mix_kernel, tm=tm)
    row = lambda i: (0, 0)
    before = (jnp.arange(tm)[:, None] < jnp.arange(tm)[None, :]).astype(BF16)
    return pl.pallas_call(
        kern,
        grid=(n // tm,),
        in_specs=[pl.BlockSpec((tm, ko), lambda i: (i, 0)),
                  pl.BlockSpec((ko, D_MODEL), row),
                  pl.BlockSpec((tm, D_MODEL), lambda i: (i, 0)),
                  pl.BlockSpec((1, D_MODEL), row),
                  pl.BlockSpec((1, D_MODEL), row),
                  pl.BlockSpec((D_MODEL, LANES), row),
                  pl.BlockSpec((D_MODEL, LANES), row),
                  pl.BlockSpec((N_EXPERTS, 1), row),
                  pl.BlockSpec((tm, tm), row)],
        out_specs=[pl.BlockSpec((tm, D_MODEL + TAIL), lambda i: (i, 0)),
                   pl.BlockSpec((8, tm), lambda i: (0, i)),
                   pl.BlockSpec((CNT_ROWS, LANES), row)],
        out_shape=[jax.ShapeDtypeStruct((n, D_MODEL + TAIL), F32),
                   jax.ShapeDtypeStruct((8, n), F32),
                   jax.ShapeDtypeStruct((CNT_ROWS, LANES), F32)],
        compiler_params=_cparams("arbitrary"),
        name="post_mix",
    )(o, w_out, x, ln_g, ln_b, rw_hi, rw_lo, rb_col, before)


PERM_CHUNK = 2048
PERM_UNROLL = 8


def _row_copy(src_ref, dst_ref, sem, s, d):
    return pltpu.make_async_copy(src_ref.at[pl.ds(s, 1)], dst_ref.at[pl.ds(d, 1)], sem)


def _permute_rows_loops(copy_row):
    def issue(i, _):
        for u in range(PERM_UNROLL):
            copy_row(i * PERM_UNROLL + u).start()
        return 0

    def drain(i, _):
        for u in range(PERM_UNROLL):
            copy_row(0).wait()
        return 0

    lax.fori_loop(0, PERM_CHUNK // PERM_UNROLL, issue, 0)
    lax.fori_loop(0, PERM_CHUNK // PERM_UNROLL, drain, 0)


def _scatter_rows_kernel(idx_ref, src_ref, init_ref, out_ref, sem):
    del init_ref
    _permute_rows_loops(lambda r: _row_copy(src_ref, out_ref, sem, r, idx_ref[r]))


def _scatter_rows(src, dest, n_out):
    n, w = src.shape
    return pl.pallas_call(
        _scatter_rows_kernel,
        grid=(n // PERM_CHUNK,),
        in_specs=[pl.BlockSpec((PERM_CHUNK,), lambda i: (i,), memory_space=pltpu.SMEM),
                  pl.BlockSpec((PERM_CHUNK, w), lambda i: (i, 0)),
                  pl.BlockSpec(memory_space=pl.ANY)],
        out_specs=pl.BlockSpec(memory_space=pl.ANY),
        out_shape=jax.ShapeDtypeStruct((n_out, w), src.dtype),
        scratch_shapes=[pltpu.SemaphoreType.DMA(())],
        input_output_aliases={2: 0},
        compiler_params=pltpu.CompilerParams(dimension_semantics=("arbitrary",), has_side_effects=True),
        name="scatter_rows",
    )(dest, src, jnp.zeros((n_out, w), src.dtype))


def _gather_rows_kernel(idx_ref, src_ref, out_ref, sem):
    _permute_rows_loops(lambda r: _row_copy(src_ref, out_ref, sem, idx_ref[r], r))


def _gather_rows(src, idx):
    n = idx.shape[0]
    w = src.shape[1]
    return pl.pallas_call(
        _gather_rows_kernel,
        grid=(n // PERM_CHUNK,),
        in_specs=[pl.BlockSpec((PERM_CHUNK,), lambda i: (i,), memory_space=pltpu.SMEM),
                  pl.BlockSpec(memory_space=pl.ANY)],
        out_specs=pl.BlockSpec((PERM_CHUNK, w), lambda i: (i, 0)),
        out_shape=jax.ShapeDtypeStruct((n, w), src.dtype),
        scratch_shapes=[pltpu.SemaphoreType.DMA(())],
        compiler_params=pltpu.CompilerParams(dimension_semantics=("arbitrary",), has_side_effects=True),
        name="gather_rows",
    )(idx, src)


MOE_TILE = 256


def _moe_kernel(src_ref, ea_ref, eb_ref, nused_ref, xs_ref, gua_ref, gub_ref, da_ref, db_ref, y_ref):
    del src_ref, ea_ref, eb_ref

    @pl.when(pl.program_id(0) >= nused_ref[0])
    def _():
        y_ref[...] = jnp.zeros_like(y_ref)

    @pl.when(pl.program_id(0) < nused_ref[0])
    def _():
        x = xs_ref[:, :D_MODEL].astype(BF16)
        tail = xs_ref[:, D_MODEL:]
        lane = lax.broadcasted_iota(jnp.int32, tail.shape, 1)
        w_lo = jnp.sum(jnp.where(lane == 0, tail, 0.0), axis=1, keepdims=True)
        w_hi = jnp.sum(jnp.where(lane == 1, tail, 0.0), axis=1, keepdims=True)

        def expert(gu_ref, d_ref):
            gu = _dot(x, gu_ref[...])
            g_in, u_in = gu[:, :D_EXPERT], gu[:, D_EXPERT:]
            act = (g_in * jax.nn.sigmoid(g_in)) * u_in
            return _dot(act.astype(BF16), d_ref[...])

        y_ref[...] = w_lo * expert(gua_ref, da_ref) + w_hi * expert(gub_ref, db_ref)


def _moe_ffn(xs, tile_src, tile_ea, tile_eb, n_used, w_gate_up, w_down):
    n_pad = xs.shape[0]
    n_tiles = n_pad // MOE_TILE
    grid_spec = pltpu.PrefetchScalarGridSpec(
        num_scalar_prefetch=4,
        grid=(n_tiles,),
        in_specs=[pl.BlockSpec((MOE_TILE, D_MODEL + TAIL), lambda t, src, ea, eb, nu: (src[t], 0)),
                  pl.BlockSpec((None, D_MODEL, 2 * D_EXPERT), lambda t, src, ea, eb, nu: (ea[t], 0, 0)),
                  pl.BlockSpec((None, D_MODEL, 2 * D_EXPERT), lambda t, src, ea, eb, nu: (eb[t], 0, 0)),
                  pl.BlockSpec((None, D_EXPERT, D_MODEL), lambda t, src, ea, eb, nu: (ea[t], 0, 0)),
                  pl.BlockSpec((None, D_EXPERT, D_MODEL), lambda t, src, ea, eb, nu: (eb[t], 0, 0))],
        out_specs=pl.BlockSpec((MOE_TILE, D_MODEL), lambda t, src, ea, eb, nu: (t, 0)),
    )
    return pl.pallas_call(
        _moe_kernel,
        grid_spec=grid_spec,
        out_shape=jax.ShapeDtypeStruct((n_pad, D_MODEL), F32),
        compiler_params=_cparams("arbitrary"),
        name="moe_ffn",
    )(tile_src, tile_ea, tile_eb, n_used, xs, w_gate_up, w_gate_up, w_down, w_down)


def _routing_plan(rec, counts, n):
    n_tiles = n // MOE_TILE + N_BUCKETS
    cnt = counts[:N_BUCKETS, 0].astype(jnp.int32)
    tiles_per = (cnt + MOE_TILE - 1) // MOE_TILE
    tile_end = jnp.cumsum(tiles_per)
    tile_start = tile_end - tiles_per
    bucket = rec[2].astype(jnp.int32)
    rank = rec[3].astype(jnp.int32)
    dest = tile_start[bucket] * MOE_TILE + rank
    n_used = tile_end[-1]
    tile_id = jnp.clip(jnp.arange(n_tiles, dtype=jnp.int32), 0, jnp.maximum(n_used - 1, 0))
    tile_bucket = jnp.sum(tile_id[:, None] >= tile_end[None, :], axis=1).astype(jnp.int32)
    group = tile_bucket // 6
    pair = tile_bucket % 6
    lo = jnp.where(pair < 3, 0, jnp.where(pair < 5, 1, 2))
    hi = jnp.where(pair < 3, pair + 1, jnp.where(pair < 5, pair - 1, 3))
    return dest, tile_id, group * 4 + lo, group * 4 + hi, n_used.reshape(1).astype(jnp.int32), n_tiles * MOE_TILE


def _final_kernel(hx_ref, moe_ref, p_ref, wg_ref, wp_ref, g_ref, b_ref, o_ref):
    h = hx_ref[...]
    gate = jax.nn.sigmoid(_dot(h.astype(BF16), wg_ref[...]))
    ple = gate * _dot(p_ref[...].astype(BF16), wp_ref[...])
    o_ref[...] = _layer_norm(DEEPNORM_ALPHA * h + moe_ref[...] + ple, g_ref[...], b_ref[...])


def _final(hx, moe, p, w_gate, w_proj, ln_g, ln_b):
    n = moe.shape[0]
    tm = 512
    row = lambda i: (0, 0)
    return pl.pallas_call(
        _final_kernel,
        grid=(n // tm,),
        in_specs=[pl.BlockSpec((tm, D_MODEL), lambda i: (i, 0)),
                  pl.BlockSpec((tm, D_MODEL), lambda i: (i, 0)),
                  pl.BlockSpec((tm, PLE_DIM), lambda i: (i, 0)),
                  pl.BlockSpec((D_MODEL, D_MODEL), row),
                  pl.BlockSpec((PLE_DIM, D_MODEL), row),
                  pl.BlockSpec((1, D_MODEL), row),
                  pl.BlockSpec((1, D_MODEL), row)],
        out_specs=pl.BlockSpec((tm, D_MODEL), lambda i: (i, 0)),
        out_shape=jax.ShapeDtypeStruct((n, D_MODEL), F32),
        compiler_params=_cparams("parallel"),
        name="final",
    )(hx, moe, p, w_gate, w_proj, ln_g, ln_b)


def _pad_lanes(a, width=LANES):
    return jnp.pad(a, [(0, 0)] * (a.ndim - 1) + [(0, width - a.shape[-1])])


def _moe_and_norm(o, w_out, x, p_i, ln1_g, ln1_b, router, w_gate_up, w_down, ple_w_gate, ple_w_proj,
                  ln2_g, ln2_b):
    n = x.shape[0]
    hx, rec, counts = _post_mix(o, w_out.astype(BF16), x, ln1_g[None], ln1_b[None], *router)
    dest, tile_src, tile_ea, tile_eb, n_used, n_pad = _routing_plan(rec, counts, n)
    xs = _scatter_rows(hx, dest, n_pad)
    ys = _moe_ffn(xs, tile_src, tile_ea, tile_eb, n_used, w_gate_up.astype(BF16), w_down.astype(BF16))
    moe = _gather_rows(ys, dest)
    return _final(hx, moe, p_i, ple_w_gate.astype(BF16), ple_w_proj.astype(BF16), ln2_g[None], ln2_b[None])


def kernel(x, p, sb_w_in, sb_w_out, gdn_w_in, gdn_conv_w, gdn_a_log, gdn_dt_bias, gdn_norm_w, gdn_w_out,
           ln1_g, ln1_b, router_w, router_b, expert_w_gate_up, expert_w_down, ple_w_gate, ple_w_proj,
           ln2_g, ln2_b):
    bsz, t, d = x.shape
    n = bsz * t
    x = x.reshape(n, d)
    p = p.reshape(DEPTH, n, PLE_DIM)
    rw_pad = _pad_lanes(router_w)
    rw_hi = rw_pad.astype(BF16)
    router = (rw_hi, (rw_pad - rw_hi.astype(F32)).astype(BF16), router_b[:, None])

    qkv = _linear(x, sb_w_in[0].astype(BF16), BF16, 1024, 1024)
    o = _sb_attention(qkv.reshape(bsz, t, 3 * d), bsz, t).reshape(n, d)
    x = _moe_and_norm(o, sb_w_out[0], x, p[0], ln1_g[0], ln1_b[0], router, expert_w_gate_up[0],
                      expert_w_down[0], ple_w_gate[0], ple_w_proj[0], ln2_g[0], ln2_b[0])

    w_in = gdn_w_in[0]
    split = GDN_QKV_DIM + GDN_VAL_DIM
    proj = _linear(x, w_in[:, :split].astype(BF16), BF16, 1024, 1024)
    ba = _linear(x, _pad_lanes(w_in[:, split:]).astype(BF16), F32, 1024, LANES)
    proj = proj.reshape(bsz, t, split)
    qkv_c, kt = _gdn_conv(proj, gdn_conv_w[0], bsz, t)
    head_pad = lambda a: jnp.pad(a, (GDN_V_HEADS, LANES - 2 * GDN_V_HEADS))[None]
    gb, gbt = _gdn_gate_prep(ba.reshape(bsz, t, LANES), head_pad(gdn_a_log[0]), head_pad(gdn_dt_bias[0]), bsz, t)
    o = _gdn_recurrence(qkv_c, kt, proj, gb, gbt, gdn_norm_w[0][None], bsz, t).reshape(n, GDN_VAL_DIM)
    x = _moe_and_norm(o, gdn_w_out[0], x, p[1], ln1_g[1], ln1_b[1], router, expert_w_gate_up[1],
                      expert_w_down[1], ple_w_gate[1], ple_w_proj[1], ln2_g[1], ln2_b[1])
    return x.reshape(bsz, t, d)
```
